```python
import jax, jax.numpy as jnp
from jax import lax
import numpy as np

D_MODEL = 2048
BATCH = 2
SEQ = 16384
DEPTH = 2

GLA_HEADS = 4
GLA_DK = 64
GLA_DV = 128
GLA_GATE_RANK = 16
GLA_GATE_TEMP = 16.0
HGRN_HEADS = 4
HGRN_DK = 128
HGRN_DV = 128
MLA_HEADS = 8
MLA_Q_RANK = 512
MLA_KV_RANK = 512
MLA_NOPE = 128
MLA_ROPE = 64
MLA_DV = 128
ROPE_THETA = 10000.0
GLA_WIDTH = GLA_HEADS * GLA_DV
HGRN_WIDTH = HGRN_HEADS * HGRN_DV
MLA_WIDTH = MLA_HEADS * MLA_DV
MIX_WIDTH = GLA_WIDTH + HGRN_WIDTH + MLA_WIDTH
IN_SPLITS = (
    GLA_HEADS * GLA_DK,
    GLA_HEADS * GLA_DK,
    GLA_WIDTH,
    GLA_GATE_RANK,
    GLA_GATE_RANK,
    GLA_WIDTH,
    HGRN_HEADS * HGRN_DK,
    HGRN_HEADS * HGRN_DK,
    HGRN_HEADS * HGRN_DK,
    HGRN_WIDTH,
    HGRN_WIDTH,
    MLA_Q_RANK,
    MLA_KV_RANK,
    MLA_ROPE,
)
D_IN = sum(IN_SPLITS)
D_FF = ((8 * D_MODEL // 3 + 255) // 256) * 256
CHUNK = 64
Q_BLOCK = 128
EPS = 1e-6

kernel_name = "hybrid_gla_hgrn2_mla_parallel_heads_encoder"


def rmsnorm(x, g):
    xf = x.astype(jnp.float32)
    y = xf * lax.rsqrt(jnp.mean(xf * xf, axis=-1, keepdims=True) + EPS) * g.astype(jnp.float32)
    return y.astype(x.dtype)


def to_heads(a, n_heads):
    b, t, _ = a.shape
    return a.reshape(b, t, n_heads, -1).transpose(0, 2, 1, 3).astype(jnp.float32)


def chunk_gated_scan(q, k, v, log_g):
    b, h, t, dk = q.shape
    dv = v.shape[-1]
    n = t // CHUNK

    def to_chunks(a):
        return a.reshape(b, h, n, CHUNK, a.shape[-1]).transpose(2, 0, 1, 3, 4)

    qc, kc, vc = to_chunks(q), to_chunks(k), to_chunks(v)
    gc = jnp.cumsum(to_chunks(log_g), axis=3)
    mask = jnp.tril(jnp.ones((CHUNK, CHUNK), dtype=bool))

    def step(state, inp):
        q_, k_, v_, g_ = inp
        diff = g_[:, :, :, None, :] - g_[:, :, None, :, :]
        decay = jnp.exp(jnp.where(mask[:, :, None], diff, -jnp.inf))
        att = jnp.einsum('bhid,bhjd,bhijd->bhij', q_, k_, decay)
        o = (jnp.einsum('bhij,bhjv->bhiv', att, v_)
             + jnp.einsum('bhid,bhdv->bhiv', q_ * jnp.exp(g_), state))
        g_last = g_[:, :, -1:, :]
        k_dec = k_ * jnp.exp(g_last - g_)
        state = (state * jnp.exp(g_last[:, :, 0, :])[..., None]
                 + jnp.einsum('bhjd,bhjv->bhdv', k_dec, v_))
        return state, o

    s0 = jnp.zeros((b, h, dk, dv), jnp.float32)
    _, o = lax.scan(step, s0, (qc, kc, vc, gc))
    return o.transpose(1, 2, 0, 3, 4).reshape(b, h, t, dv)


def bidir_gated(q, k_f, k_b, v, lg_f, lg_b):
    def flip(a):
        return jnp.flip(a, axis=2)
    o_f = chunk_gated_scan(q, k_f, v, lg_f)
    o_b = flip(chunk_gated_scan(flip(q), flip(k_b), flip(v), flip(lg_b)))
    return o_f + o_b


def gated_out_norm(o, u_g, gain, n_heads):
    b, h, t, dv = o.shape
    o = o.transpose(0, 2, 1, 3)
    g = u_g.astype(jnp.float32).reshape(b, t, h, dv)
    return (rmsnorm(o, gain) * jax.nn.silu(g)).reshape(b, t, h * dv)


def gla_mixer(u_q, u_k, u_v, u_af, u_ab, u_g, gate_up, gate_bias, out_norm):
    q = to_heads(u_q, GLA_HEADS) * (GLA_DK ** -0.5)
    k = to_heads(u_k, GLA_HEADS)
    v = to_heads(u_v, GLA_HEADS)

    def log_decay(u_a, d):
        z = u_a.astype(jnp.float32) @ gate_up[d].astype(jnp.float32) + gate_bias[d].astype(jnp.float32)
        return to_heads(jax.nn.log_sigmoid(z) / GLA_GATE_TEMP, GLA_HEADS)

    o = bidir_gated(q, k, k, v, log_decay(u_af, 0), log_decay(u_ab, 1))
    return gated_out_norm(o, u_g, out_norm, GLA_HEADS).astype(u_q.dtype)


def hgrn2_mixer(u_q, u_ff, u_fb, u_i, u_g, lower_bound, out_norm):
    q = to_heads(jax.nn.silu(u_q.astype(jnp.float32)), HGRN_HEADS)
    v = to_heads(u_i, HGRN_HEADS)

    def gates(u_f, lb):
        z = to_heads(u_f, HGRN_HEADS)
        lb = lb.reshape(1, HGRN_HEADS, 1, HGRN_DK)
        key = (1.0 - lb) * jax.nn.sigmoid(-z)
        log_f = jnp.logaddexp(jnp.log(lb), jnp.log1p(-lb) + jax.nn.log_sigmoid(z))
        return key, log_f

    k_f, lf_f = gates(u_ff, lower_bound[0])
    k_b, lf_b = gates(u_fb, lower_bound[1])
    o = bidir_gated(q, k_f, k_b, v, lf_f, lf_b)
    return gated_out_norm(o, u_g, out_norm, HGRN_HEADS).astype(u_q.dtype)


def rope(x, cos, sin):
    x1, x2 = jnp.split(x, 2, axis=-1)
    return jnp.concatenate([x1 * cos - x2 * sin, x2 * cos + x1 * sin], axis=-1)


def mla_mixer(u_cq, u_ckv, u_kr, cos, sin, qa_norm, w_qb, kva_norm, w_kvb):
    b, t, _ = u_cq.shape
    dqk = MLA_NOPE + MLA_ROPE
    q = (rmsnorm(u_cq, qa_norm) @ w_qb).reshape(b, t, MLA_HEADS, dqk)
    q_rope = rope(q[..., MLA_NOPE:], cos[:, :, None, :], sin[:, :, None, :])
    q = jnp.concatenate([q[..., :MLA_NOPE], q_rope], axis=-1)
    kv = (rmsnorm(u_ckv, kva_norm) @ w_kvb).reshape(b, t, MLA_HEADS, MLA_NOPE + MLA_DV)
    k_rope = rope(u_kr, cos, sin)
    k = jnp.concatenate(
        [kv[..., :MLA_NOPE], jnp.broadcast_to(k_rope[:, :, None, :], (b, t, MLA_HEADS, MLA_ROPE))], axis=-1)
    kh = k.transpose(0, 2, 1, 3)
    vh = kv[..., MLA_NOPE:].transpose(0, 2, 1, 3)
    scale = dqk ** -0.5
    nb = t // Q_BLOCK
    qb = q.reshape(b, nb, Q_BLOCK, MLA_HEADS, dqk).transpose(1, 0, 3, 2, 4)

    def attend(qi):
        s = jnp.einsum('bhqd,bhkd->bhqk', qi, kh).astype(jnp.float32) * scale
        p = jax.nn.softmax(s, axis=-1).astype(vh.dtype)
        return jnp.einsum('bhqk,bhkd->bhqd', p, vh)

    o = lax.map(attend, qb)
    return o.transpose(1, 0, 3, 2, 4).reshape(b, t, MLA_WIDTH)


def swiglu(h, w1, w3, w2):
    return (jax.nn.silu(h @ w1) * (h @ w3)) @ w2


def setup_inputs(seed: int = 0) -> dict:
    key = jax.random.key(seed)
    ks = jax.random.split(key, 20)
    f32 = jnp.float32

    def nrm(k, shape, fan_in):
        return jax.random.normal(k, shape, f32) * (fan_in ** -0.5)

    def gain(k, shape):
        return 1.0 + 0.02 * jax.random.normal(k, shape, f32)

    x = jax.random.normal(ks[0], (BATCH, SEQ, D_MODEL), f32)
    offsets = jax.random.randint(ks[1], (BATCH, 1), 0, 4096)
    positions = (jnp.arange(SEQ, dtype=jnp.int32)[None, :] + offsets).astype(jnp.int32)
    return {
        "x": x,
        "positions": positions,
        "w_in": nrm(ks[2], (DEPTH, D_MODEL, D_IN), D_MODEL),
        "gla_gate_up": nrm(ks[3], (DEPTH, 2, GLA_GATE_RANK, GLA_HEADS * GLA_DK), GLA_GATE_RANK),
        "gla_gate_bias": 0.1 * jax.random.normal(ks[4], (DEPTH, 2, GLA_HEADS * GLA_DK), f32),
        "gla_out_norm": gain(ks[5], (DEPTH, GLA_DV)),
        "hgrn_lb_logits": 0.5 * jax.random.normal(ks[6], (2, DEPTH, HGRN_HEADS * HGRN_DK), f32),
        "hgrn_out_norm": gain(ks[7], (DEPTH, HGRN_DV)),
        "mla_qa_norm": gain(ks[8], (DEPTH, MLA_Q_RANK)),
        "mla_w_qb": nrm(ks[9], (DEPTH, MLA_Q_RANK, MLA_HEADS * (MLA_NOPE + MLA_ROPE)), MLA_Q_RANK),
        "mla_kva_norm": gain(ks[10], (DEPTH, MLA_KV_RANK)),
        "mla_w_kvb": nrm(ks[11], (DEPTH, MLA_KV_RANK, MLA_HEADS * (MLA_NOPE + MLA_DV)), MLA_KV_RANK),
        "w_out": nrm(ks[12], (DEPTH, MIX_WIDTH, D_MODEL), MIX_WIDTH),
        "norm_mix": gain(ks[13], (DEPTH, D_MODEL)),
        "norm_ffn": gain(ks[14], (DEPTH, D_MODEL)),
        "ffn_w1": nrm(ks[15], (DEPTH, D_MODEL, D_FF), D_MODEL),
        "ffn_w3": nrm(ks[16], (DEPTH, D_MODEL, D_FF), D_MODEL),
        "ffn_w2": nrm(ks[17], (DEPTH, D_FF, D_MODEL), D_FF),
        "norm_final": gain(ks[18], (D_MODEL,)),
    }


def reference(x, positions, w_in, gla_gate_up, gla_gate_bias, gla_out_norm, hgrn_lb_logits,
              hgrn_out_norm, mla_qa_norm, mla_w_qb, mla_kva_norm, mla_w_kvb, w_out,
              norm_mix, norm_ffn, ffn_w1, ffn_w3, ffn_w2, norm_final):
    inv_freq = 1.0 / (ROPE_THETA ** (jnp.arange(0, MLA_ROPE, 2, dtype=jnp.float32) / MLA_ROPE))
    ang = positions.astype(jnp.float32)[..., None] * inv_freq
    cos = jnp.cos(ang).astype(x.dtype)
    sin = jnp.sin(ang).astype(x.dtype)
    lb_cum = jnp.cumsum(jax.nn.softmax(hgrn_lb_logits.astype(jnp.float32), axis=1), axis=1)
    lower_bounds = lb_cum - lb_cum[:, :1]
    split_idx = np.cumsum(IN_SPLITS)[:-1].tolist()

    h = x
    for l in range(DEPTH):
        u = rmsnorm(h, norm_mix[l]) @ w_in[l]
        (gq, gk, gv, gaf, gab, gg, hq, hff, hfb, hi, hg, cq, ckv, kr) = jnp.split(u, split_idx, axis=-1)
        y_gla = gla_mixer(gq, gk, gv, gaf, gab, gg, gla_gate_up[l], gla_gate_bias[l], gla_out_norm[l])
        y_hgrn = hgrn2_mixer(hq, hff, hfb, hi, hg, lower_bounds[:, l], hgrn_out_norm[l])
        y_mla = mla_mixer(cq, ckv, kr, cos, sin, mla_qa_norm[l], mla_w_qb[l], mla_kva_norm[l], mla_w_kvb[l])
        mixed = jnp.concatenate([y_gla, y_hgrn, y_mla.astype(h.dtype)], axis=-1)
        h = h + mixed @ w_out[l]
        h = h + swiglu(rmsnorm(h, norm_ffn[l]), ffn_w1[l], ffn_w3[l], ffn_w2[l])
    return rmsnorm(h, norm_final)
```

```python
import functools

import jax
import jax.numpy as jnp
import numpy as np
from jax import lax
from jax.experimental import pallas as pl
from jax.experimental.pallas import tpu as pltpu

D_MODEL = 2048
GLA_HEADS, GLA_DK, GLA_DV = 4, 64, 128
GLA_GATE_RANK = 16
GLA_GATE_TEMP = 16.0
HGRN_HEADS, HGRN_DK, HGRN_DV = 4, 128, 128
MLA_HEADS, MLA_Q_RANK, MLA_KV_RANK = 8, 512, 512
MLA_NOPE, MLA_ROPE, MLA_DV = 128, 64, 128
MLA_DQK = MLA_NOPE + MLA_ROPE
ROPE_THETA = 10000.0
EPS = 1e-6

LANES = 128
SCAN_HEADS = GLA_HEADS + HGRN_HEADS
SCAN_WIDTH = SCAN_HEADS * LANES
GLA_W = GLA_HEADS * LANES
HGRN_W = HGRN_HEADS * LANES

CB_GQ, CB_GK, CB_GV, CB_GG, CB_HQ, CB_HFF, CB_HFB, CB_HI, CB_HG, CB_CQ, CB_CKV = range(11)
CB128_KR = 44
CB128_GA = 45
U_WIDTH = 6144

SCAN_CHUNK = 16
SCAN_BLOCK = 256
VMEM_LIMIT = 56 * 1024 * 1024

F32 = jnp.float32
BF16 = jnp.bfloat16


def _cparams(sem):
    return pltpu.CompilerParams(dimension_semantics=sem, vmem_limit_bytes=VMEM_LIMIT)


def _rms(x, gain):
    return x * lax.rsqrt(jnp.mean(x * x, axis=-1, keepdims=True) + EPS) * gain


def _sigmoid(z):
    return 1.0 / (1.0 + jnp.exp(-z))


def _log_sigmoid(z):
    return jnp.minimum(z, 0.0) - jnp.log1p(jnp.exp(-jnp.abs(z)))


def _inproj_kernel(x_ref, g_ref, w_ref, o_ref, xn_ref):
    @pl.when(pl.program_id(1) == 0)
    def _():
        xn_ref[...] = _rms(x_ref[...], g_ref[...]).astype(BF16)

    o_ref[...] = jnp.dot(xn_ref[...], w_ref[...], preferred_element_type=F32)


def _inproj(h2d, gain, w, tm=512, tn=1536):
    m, d = h2d.shape
    n = w.shape[1]
    return pl.pallas_call(
        _inproj_kernel,
        grid=(m // tm, n // tn),
        in_specs=[
            pl.BlockSpec((tm, d), lambda i, j: (i, 0)),
            pl.BlockSpec((1, d), lambda i, j: (0, 0)),
            pl.BlockSpec((d, tn), lambda i, j: (0, j)),
        ],
        out_specs=pl.BlockSpec((tm, tn), lambda i, j: (i, j)),
        out_shape=jax.ShapeDtypeStruct((m, n), F32),
        scratch_shapes=[pltpu.VMEM((tm, d), BF16)],
        compiler_params=_cparams(("parallel", "arbitrary")),
        name="inproj",
    )(h2d, gain, w)


def _scan_kernel(reverse, layer, *refs):
    if reverse:
        (gq, gk, gv, ga, hq, hf, hi, wg, gb, lbl, tri,
         out, qs, ks, gs, st) = refs
    else:
        (gq, gk, gv, gg, ga, hq, hf, hi, hg, ob, wg, gb, lbl, tri, onorm,
         out, qs, ks, gs, st, osum) = refs
    tb = qs.shape[0]
    n_chunks = tb // SCAN_CHUNK
    c = SCAN_CHUNK

    @pl.when(pl.program_id(1) == 0)
    def _():
        st[...] = jnp.zeros_like(st)

    qs[:, :GLA_W] = gq[...] * (GLA_DK ** -0.5)
    hqv = hq[...]
    qs[:, GLA_W:] = hqv * _sigmoid(hqv)
    z = jnp.dot(ga[...], wg[...], preferred_element_type=F32,
                precision=lax.Precision.HIGHEST) + gb[...]
    lg_gla = _log_sigmoid(z) * (1.0 / GLA_GATE_TEMP)
    logits = lbl[...]
    e = jnp.exp(logits - jnp.max(logits, axis=0, keepdims=True))
    sm = e / jnp.sum(e, axis=0, keepdims=True)
    lb = jnp.zeros_like(sm[0:1])
    for dd in range(1, layer + 1):
        lb = lb + sm[dd:dd + 1]
    zf = hf[...]
    ks[:, :GLA_W] = gk[...]
    ks[:, GLA_W:] = (1.0 - lb) * _sigmoid(-zf)
    a = jnp.log(lb)
    b = jnp.log1p(-lb) + _log_sigmoid(zf)
    lg_h = jnp.maximum(a, b) + jnp.log1p(jnp.exp(-jnp.abs(a - b)))
    gs[:, :GLA_W] = jnp.dot(tri[...], lg_gla, preferred_element_type=F32,
                            precision=lax.Precision.HIGHEST)
    gs[:, GLA_W:] = jnp.dot(tri[...], lg_h, preferred_element_type=F32,
                            precision=lax.Precision.HIGHEST)

    row = lax.broadcasted_iota(jnp.int32, (c, LANES), 0)
    ones = jnp.ones((LANES, LANES), BF16)
    o_dst = out if reverse else osum

    def chunk_body(ci, carry):
        cidx = (n_chunks - 1 - ci) if reverse else ci
        r = pl.multiple_of(cidx * c, c)
        for h in range(SCAN_HEADS):
            hl = slice(h * LANES, (h + 1) * LANES)
            q = qs[pl.ds(r, c), hl]
            k = ks[pl.ds(r, c), hl]
            g = gs[pl.ds(r, c), hl]
            if h < GLA_HEADS:
                v = gv[pl.ds(r, c), hl]
            else:
                v = hi[pl.ds(r, c), (h - GLA_HEADS) * LANES:(h - GLA_HEADS + 1) * LANES]
            g_edge = g[0:1] if reverse else g[c - 1:c]
            s_t = st[h]
            o_inter = lax.dot_general(
                (q * jnp.exp(g)).astype(BF16), s_t.astype(BF16),
                (((1,), (1,)), ((), ())), preferred_element_type=F32)
            parts = []
            for j in range(c):
                live = (row <= j) if reverse else (row >= j)
                dec = jnp.exp(jnp.where(live, g - g[j:j + 1], -jnp.inf))
                parts.append((q * dec * k[j:j + 1]).astype(BF16))
            att = jnp.dot(jnp.concatenate(parts, axis=0), ones,
                          preferred_element_type=F32)
            o_intra = att[0:c] * v[0:1]
            for j in range(1, c):
                o_intra = o_intra + att[j * c:(j + 1) * c] * v[j:j + 1]
            o_dst[pl.ds(r, c), hl] = o_inter + o_intra
            k_dec = (k * jnp.exp(g_edge - g)).astype(BF16)
            st[h] = s_t * jnp.exp(g_edge) + lax.dot_general(
                v.astype(BF16), k_dec, (((0,), (0,)), ((), ())),
                preferred_element_type=F32)
        return carry

    lax.fori_loop(0, n_chunks, chunk_body, 0)

    if not reverse:
        for h in range(SCAN_HEADS):
            hl = slice(h * LANES, (h + 1) * LANES)
            o = osum[:, hl] + ob[:, hl]
            if h < GLA_HEADS:
                gate = gg[:, hl]
            else:
                gate = hg[:, (h - GLA_HEADS) * LANES:(h - GLA_HEADS + 1) * LANES]
            y = _rms(o, onorm[:, hl]) * (gate * _sigmoid(gate))
            out[:, hl] = y.astype(out.dtype)


def _scan(u, o_b, wg, gb, lbl, tri, onorm, *, batch, seq, layer, reverse):
    tb = SCAN_BLOCK
    nt = seq // tb

    def rows(b, t):
        return b * nt + ((nt - 1 - t) if reverse else t)

    def ublock(cb):
        return pl.BlockSpec((tb, 512), lambda b, t: (rows(b, t), cb))

    def const(shape):
        return pl.BlockSpec(shape, lambda b, t: (0,) * len(shape))

    ga_spec = pl.BlockSpec((tb, LANES), lambda b, t: (rows(b, t), CB128_GA))
    wide = pl.BlockSpec((tb, SCAN_WIDTH), lambda b, t: (rows(b, t), 0))
    params = [const((LANES, GLA_W)), const((1, GLA_W)), const(lbl.shape), const((tb, tb))]
    scratch = [pltpu.VMEM((tb, SCAN_WIDTH), F32)] * 3 + [pltpu.VMEM((SCAN_HEADS, LANES, LANES), F32)]
    if reverse:
        in_specs = [ublock(CB_GQ), ublock(CB_GK), ublock(CB_GV), ga_spec,
                    ublock(CB_HQ), ublock(CB_HFB), ublock(CB_HI)] + params
        args = [u, u, u, u, u, u, u, wg, gb, lbl, tri]
        out_dtype = F32
    else:
        in_specs = [ublock(CB_GQ), ublock(CB_GK), ublock(CB_GV), ublock(CB_GG), ga_spec,
                    ublock(CB_HQ), ublock(CB_HFF), ublock(CB_HI), ublock(CB_HG), wide] + params + [
                        const((1, SCAN_WIDTH))]
        args = [u, u, u, u, u, u, u, u, u, o_b, wg, gb, lbl, tri, onorm]
        out_dtype = BF16
        scratch = scratch + [pltpu.VMEM((tb, SCAN_WIDTH), F32)]
    return pl.pallas_call(
        functools.partial(_scan_kernel, reverse, layer),
        grid=(batch, nt),
        in_specs=in_specs,
        out_specs=wide,
        out_shape=jax.ShapeDtypeStruct((batch * seq, SCAN_WIDTH), out_dtype),
        scratch_shapes=scratch,
        compiler_params=_cparams(("arbitrary", "arbitrary")),
        name="scan_bwd" if reverse else "scan_fwd",
    )(*args)


def _mla_prep_kernel(cq_ref, ckv_ref, kr_ref, cs_ref, gq_ref, gkv_ref, wq_ref, wkv_ref,
                     q_out, k_out, vt_out):
    scale = MLA_DQK ** -0.5
    cs = cs_ref[...]
    yq = jnp.dot(_rms(cq_ref[...], gq_ref[...]).astype(BF16), wq_ref[...],
                 preferred_element_type=F32)
    ykv = jnp.dot(_rms(ckv_ref[...], gkv_ref[...]).astype(BF16), wkv_ref[...],
                  preferred_element_type=F32)

    def rotary(t):
        t = t * cs
        return (t + pltpu.roll(t, MLA_ROPE, 1))[:, :MLA_ROPE]

    k_rope = rotary(kr_ref[...]).astype(BF16)
    for h in range(MLA_HEADS):
        yh = yq[:, h * 256:(h + 1) * 256]
        q_out[h, :, :MLA_NOPE] = (yh[:, :MLA_NOPE] * scale).astype(BF16)
        q_out[h, :, MLA_NOPE:] = (rotary(yh[:, MLA_NOPE:]) * scale).astype(BF16)
        kvh = ykv[:, h * 256:(h + 1) * 256]
        k_out[h, :, :MLA_NOPE] = kvh[:, :MLA_NOPE].astype(BF16)
        k_out[h, :, MLA_NOPE:] = k_rope
        vt_out[h] = kvh[:, MLA_NOPE:].T.astype(BF16)


def _mla_prep(u, cs, gq, gkv, wq, wkv, *, batch, seq, tm=512):
    nt = seq // tm

    def rows(b, t):
        return b * nt + t

    def const(shape):
        return pl.BlockSpec(shape, lambda b, t: (0,) * len(shape))

    return pl.pallas_call(
        _mla_prep_kernel,
        grid=(batch, nt),
        in_specs=[
            pl.BlockSpec((tm, 512), lambda b, t: (rows(b, t), CB_CQ)),
            pl.BlockSpec((tm, 512), lambda b, t: (rows(b, t), CB_CKV)),
            pl.BlockSpec((tm, LANES), lambda b, t: (rows(b, t), CB128_KR)),
            pl.BlockSpec((tm, LANES), lambda b, t: (rows(b, t), 0)),
            const((1, MLA_Q_RANK)), const((1, MLA_KV_RANK)),
            const(wq.shape), const(wkv.shape),
        ],
        out_specs=[
            pl.BlockSpec((None, MLA_HEADS, tm, MLA_DQK), lambda b, t: (b, 0, t, 0)),
            pl.BlockSpec((None, MLA_HEADS, tm, MLA_DQK), lambda b, t: (b, 0, t, 0)),
            pl.BlockSpec((None, MLA_HEADS, MLA_DV, tm), lambda b, t: (b, 0, 0, t)),
        ],
        out_shape=[
            jax.ShapeDtypeStruct((batch, MLA_HEADS, seq, MLA_DQK), BF16),
            jax.ShapeDtypeStruct((batch, MLA_HEADS, seq, MLA_DQK), BF16),
            jax.ShapeDtypeStruct((batch, MLA_HEADS, MLA_DV, seq), BF16),
        ],
        compiler_params=_cparams(("parallel", "parallel")),
        name="mla_prep",
    )(u, u, u, cs, gq, gkv, wq, wkv)


def _attn_kernel(tk, q_ref, k_ref, vt_ref, o_ref, m_ref, l_ref, acc_ref):
    n_k = k_ref.shape[0] // tk
    q = q_ref[...]
    m_ref[...] = jnp.full_like(m_ref, -jnp.inf)
    l_ref[...] = jnp.zeros_like(l_ref)
    acc_ref[...] = jnp.zeros_like(acc_ref)

    def body(i, carry):
        start = pl.multiple_of(i * tk, tk)
        s = lax.dot_general(k_ref[pl.ds(start, tk), :], q, (((1,), (1,)), ((), ())),
                            preferred_element_type=F32)
        m_old = m_ref[...]
        m_new = jnp.maximum(m_old, jnp.max(s, axis=0, keepdims=True))
        alpha = jnp.exp(m_old - m_new)
        p = jnp.exp(s - m_new)
        l_ref[...] = alpha * l_ref[...] + jnp.sum(p, axis=0, keepdims=True)
        acc_ref[...] = alpha * acc_ref[...] + jnp.dot(
            vt_ref[:, pl.ds(start, tk)], p.astype(BF16), preferred_element_type=F32)
        m_ref[...] = m_new
        return carry

    lax.fori_loop(0, n_k, body, 0)
    o_ref[...] = (acc_ref[...] / l_ref[...]).T.astype(o_ref.dtype)


def _attention(q, k, vt, *, tq=512, tk=512):
    batch, heads, seq, dqk = q.shape
    dv = vt.shape[2]
    return pl.pallas_call(
        functools.partial(_attn_kernel, tk),
        grid=(batch, heads, seq // tq),
        in_specs=[
            pl.BlockSpec((None, None, tq, dqk), lambda b, h, i: (b, h, i, 0)),
            pl.BlockSpec((None, None, seq, dqk), lambda b, h, i: (b, h, 0, 0)),
            pl.BlockSpec((None, None, dv, seq), lambda b, h, i: (b, h, 0, 0)),
        ],
        out_specs=pl.BlockSpec((None, tq, dv), lambda b, h, i: (b, i, h)),
        out_shape=jax.ShapeDtypeStruct((batch, seq, heads * dv), BF16),
        scratch_shapes=[pltpu.VMEM((1, tq), F32), pltpu.VMEM((1, tq), F32),
                        pltpu.VMEM((dv, tq), F32)],
        compiler_params=_cparams(("parallel", "parallel", "arbitrary")),
        name="mla_attn",
    )(q, k, vt)


def _outproj_kernel(h_ref, ys_ref, ym_ref, w_ref, o_ref):
    ws = ys_ref.shape[1]
    o_ref[...] = (h_ref[...]
                  + jnp.dot(ys_ref[...], w_ref[:ws, :], preferred_element_type=F32)
                  + jnp.dot(ym_ref[...], w_ref[ws:, :], preferred_element_type=F32))


def _outproj(h2d, ys, ym, w, tm=512):
    m, d = h2d.shape
    return pl.pallas_call(
        _outproj_kernel,
        grid=(m // tm,),
        in_specs=[
            pl.BlockSpec((tm, d), lambda i: (i, 0)),
            pl.BlockSpec((tm, ys.shape[1]), lambda i: (i, 0)),
            pl.BlockSpec((tm, ym.shape[1]), lambda i: (i, 0)),
            pl.BlockSpec(w.shape, lambda i: (0, 0)),
        ],
        out_specs=pl.BlockSpec((tm, d), lambda i: (i, 0)),
        out_shape=jax.ShapeDtypeStruct((m, d), F32),
        compiler_params=_cparams(("parallel",)),
        name="outproj",
    )(h2d, ys, ym, w)


def _ffn_kernel(final, h_ref, g_ref, gf_ref, w1_ref, w3_ref, w2_ref, o_ref, xn_ref, acc_ref):
    j = pl.program_id(1)

    @pl.when(j == 0)
    def _():
        xn_ref[...] = _rms(h_ref[...], g_ref[...]).astype(BF16)
        acc_ref[...] = jnp.zeros_like(acc_ref)

    xn = xn_ref[...]
    a = jnp.dot(xn, w1_ref[...], preferred_element_type=F32)
    b = jnp.dot(xn, w3_ref[...], preferred_element_type=F32)
    gated = (a * _sigmoid(a) * b).astype(BF16)
    acc_ref[...] += jnp.dot(gated, w2_ref[...], preferred_element_type=F32)

    @pl.when(j == pl.num_programs(1) - 1)
    def _():
        y = h_ref[...] + acc_ref[...]
        if final:
            y = _rms(y, gf_ref[...])
        o_ref[...] = y


def _ffn(h2d, gain, gain_final, w1, w3, w2, *, final, tm=512, tf=512):
    m, d = h2d.shape
    f = w1.shape[1]
    return pl.pallas_call(
        functools.partial(_ffn_kernel, final),
        grid=(m // tm, f // tf),
        in_specs=[
            pl.BlockSpec((tm, d), lambda i, j: (i, 0)),
            pl.BlockSpec((1, d), lambda i, j: (0, 0)),
            pl.BlockSpec((1, d), lambda i, j: (0, 0)),
            pl.BlockSpec((d, tf), lambda i, j: (0, j)),
            pl.BlockSpec((d, tf), lambda i, j: (0, j)),
            pl.BlockSpec((tf, d), lambda i, j: (j, 0)),
        ],
        out_specs=pl.BlockSpec((tm, d), lambda i, j: (i, 0)),
        out_shape=jax.ShapeDtypeStruct((m, d), F32),
        scratch_shapes=[pltpu.VMEM((tm, d), BF16), pltpu.VMEM((tm, d), F32)],
        compiler_params=_cparams(("parallel", "arbitrary")),
        name="ffn",
    )(h2d, gain, gain_final, w1, w3, w2)


def _pad_heads(w, heads, dk):
    lead = w.shape[:-1]
    w = w.reshape(lead + (heads, dk))
    w = jnp.pad(w, [(0, 0)] * len(lead) + [(0, 0), (0, LANES - dk)])
    return w.reshape(lead + (heads * LANES,))


def _layout_w_in(w):
    sizes = (256, 256, 512, 16, 16, 512, 512, 512, 512, 512, 512, 512, 512, 64)
    (gq, gk, gv, gaf, gab, gg, hq, hff, hfb, hi, hg, cq, ckv, kr) = jnp.split(
        w, np.cumsum(sizes)[:-1].tolist(), axis=1)
    half = MLA_ROPE // 2
    kr_swapped = jnp.concatenate([kr[:, half:], kr[:, :half]], axis=1)
    d = w.shape[0]
    cols = [_pad_heads(gq, GLA_HEADS, GLA_DK), _pad_heads(gk, GLA_HEADS, GLA_DK), gv, gg,
            hq, hff, hfb, hi, hg, cq, ckv, kr, kr_swapped,
            gaf, gab, jnp.zeros((d, LANES - 2 * GLA_GATE_RANK), w.dtype)]
    out = jnp.concatenate(cols, axis=1)
    out = jnp.pad(out, [(0, 0), (0, U_WIDTH - out.shape[1])])
    return out.astype(BF16)


def _layout_w_qb(w):
    r = w.shape[0]
    w = w.reshape(r, MLA_HEADS, MLA_DQK)
    rope = w[:, :, MLA_NOPE:]
    half = MLA_ROPE // 2
    swapped = jnp.concatenate([rope[:, :, half:], rope[:, :, :half]], axis=2)
    return jnp.concatenate([w, swapped], axis=2).reshape(r, MLA_HEADS * 256).astype(BF16)


def _tri(tb, reverse):
    r = np.arange(tb)[:, None]
    c = np.arange(tb)[None, :]
    same = (r // SCAN_CHUNK) == (c // SCAN_CHUNK)
    keep = (c >= r) if reverse else (c <= r)
    return jnp.asarray((same & keep).astype(np.float32))


def kernel(x, positions, w_in, gla_gate_up, gla_gate_bias, gla_out_norm, hgrn_lb_logits,
           hgrn_out_norm, mla_qa_norm, mla_w_qb, mla_kva_norm, mla_w_kvb, w_out,
           norm_mix, norm_ffn, ffn_w1, ffn_w3, ffn_w2, norm_final):
    batch, seq, d = x.shape
    depth = w_in.shape[0]
    m = batch * seq

    inv_freq = 1.0 / (ROPE_THETA ** (jnp.arange(0, MLA_ROPE, 2, dtype=F32) / MLA_ROPE))
    ang = positions.astype(F32)[..., None] * inv_freq
    cos, sin = jnp.cos(ang), jnp.sin(ang)
    cs = jnp.concatenate([cos, cos, -sin, sin], axis=-1).reshape(m, LANES)

    tri_f, tri_b = _tri(SCAN_BLOCK, False), _tri(SCAN_BLOCK, True)
    h = x.reshape(m, d)
    for l in range(depth):
        u = _inproj(h, norm_mix[l].reshape(1, d), _layout_w_in(w_in[l]))

        def gate_w(direction):
            wgd = _pad_heads(gla_gate_up[l, direction], GLA_HEADS, GLA_DK)
            lo = direction * GLA_GATE_RANK
            return jnp.pad(wgd, [(lo, LANES - lo - GLA_GATE_RANK), (0, 0)])

        gb = _pad_heads(gla_gate_bias[l], GLA_HEADS, GLA_DK)
        onorm = jnp.concatenate([jnp.tile(gla_out_norm[l], GLA_HEADS),
                                 jnp.tile(hgrn_out_norm[l], HGRN_HEADS)]).reshape(1, SCAN_WIDTH)
        o_b = _scan(u, None, gate_w(1), gb[1:2], hgrn_lb_logits[1], tri_b, None,
                    batch=batch, seq=seq, layer=l, reverse=True)
        y_scan = _scan(u, o_b, gate_w(0), gb[0:1], hgrn_lb_logits[0], tri_f, onorm,
                       batch=batch, seq=seq, layer=l, reverse=False)

        q, k, vt = _mla_prep(u, cs, mla_qa_norm[l].reshape(1, -1), mla_kva_norm[l].reshape(1, -1),
                             _layout_w_qb(mla_w_qb[l]), mla_w_kvb[l].astype(BF16),
                             batch=batch, seq=seq)
        y_mla = _attention(q, k, vt).reshape(m, MLA_HEADS * MLA_DV)

        h = _outproj(h, y_scan, y_mla, w_out[l].astype(BF16))
        h = _ffn(h, norm_ffn[l].reshape(1, d), norm_final.reshape(1, d),
                 ffn_w1[l].astype(BF16), ffn_w3[l].astype(BF16), ffn_w2[l].astype(BF16),
                 final=(l == depth - 1))
    return h.reshape(batch, seq, d)
```

```python
import functools

import jax
import jax.numpy as jnp
import numpy as np
from jax import lax
from jax.experimental import pallas as pl
from jax.experimental.pallas import tpu as pltpu

D_MODEL = 2048
GLA_HEADS, GLA_DK, GLA_DV = 4, 64, 128
GLA_GATE_RANK = 16
GLA_GATE_TEMP = 16.0
HGRN_HEADS, HGRN_DK, HGRN_DV = 4, 128, 128
MLA_HEADS, MLA_Q_RANK, MLA_KV_RANK = 8, 512, 512
MLA_NOPE, MLA_ROPE, MLA_DV = 128, 64, 128
MLA_DQK = MLA_NOPE + MLA_ROPE
ROPE_THETA = 10000.0
EPS = 1e-6

LANES = 128
MXU_TILE = 256
SCAN_HEADS = GLA_HEADS + HGRN_HEADS
SCAN_WIDTH = SCAN_HEADS * LANES
GLA_W = GLA_HEADS * LANES
HGRN_W = HGRN_HEADS * LANES

CB_GQ, CB_GK, CB_GV, CB_GG, CB_HQ, CB_HFF, CB_HFB, CB_HI, CB_HG, CB_CQ, CB_CKV = range(11)
CB128_KR = 44
CB128_GA = 45
U_WIDTH = 6144

SCAN_CHUNK = 16
SCAN_BLOCK = 256
VMEM_LIMIT = 56 * 1024 * 1024

F32 = jnp.float32
BF16 = jnp.bfloat16


def _cparams(sem):
    return pltpu.CompilerParams(dimension_semantics=sem, vmem_limit_bytes=VMEM_LIMIT)


def _rms(x, gain):
    return x * lax.rsqrt(jnp.mean(x * x, axis=-1, keepdims=True) + EPS) * gain


def _sigmoid(z):
    return 1.0 / (1.0 + jnp.exp(-z))


def _log_sigmoid(z):
    return jnp.minimum(z, 0.0) - jnp.log1p(jnp.exp(-jnp.abs(z)))


def _split_bf16(x, n):
    parts = []
    for _ in range(n):
        p = x.astype(BF16)
        parts.append(p)
        x = x - p.astype(F32)
    return parts


def _inproj_kernel(x_ref, g_ref, w_ref, o_ref, xn_ref):
    @pl.when(pl.program_id(1) == 0)
    def _():
        xn_ref[...] = _rms(x_ref[...], g_ref[...]).astype(BF16)

    o_ref[...] = jnp.dot(xn_ref[...], w_ref[...], preferred_element_type=F32)


def _inproj(h2d, gain, w, tm=512, tn=1536):
    m, d = h2d.shape
    n = w.shape[1]
    return pl.pallas_call(
        _inproj_kernel,
        grid=(m // tm, n // tn),
        in_specs=[
            pl.BlockSpec((tm, d), lambda i, j: (i, 0)),
            pl.BlockSpec((1, d), lambda i, j: (0, 0)),
            pl.BlockSpec((d, tn), lambda i, j: (0, j)),
        ],
        out_specs=pl.BlockSpec((tm, tn), lambda i, j: (i, j)),
        out_shape=jax.ShapeDtypeStruct((m, n), F32),
        scratch_shapes=[pltpu.VMEM((tm, d), BF16)],
        compiler_params=_cparams(("parallel", "arbitrary")),
        name="inproj",
    )(h2d, gain, w)


def _scan_kernel(reverse, layer, *refs):
    if reverse:
        (gq, gk, gv, ga, hq, hf, hi, wg, gb, lbl, tri,
         out, qs, ks, gs, st) = refs
    else:
        (gq, gk, gv, gg, ga, hq, hf, hi, hg, ob, wg, gb, lbl, tri, onorm,
         out, qs, ks, gs, st, osum) = refs
    tb = qs.shape[0]
    n_chunks = tb // SCAN_CHUNK
    c = SCAN_CHUNK

    @pl.when(pl.program_id(1) == 0)
    def _():
        st[...] = jnp.zeros_like(st)

    qs[:, :GLA_W] = gq[...] * (GLA_DK ** -0.5)
    hqv = hq[...]
    qs[:, GLA_W:] = hqv * _sigmoid(hqv)
    ga_hi, ga_lo = _split_bf16(ga[...], 2)
    wg_hi, wg_lo = _split_bf16(wg[...], 2)
    z = (jnp.dot(ga_hi, wg_hi, preferred_element_type=F32)
         + jnp.dot(ga_lo, wg_hi, preferred_element_type=F32)
         + jnp.dot(ga_hi, wg_lo, preferred_element_type=F32)) + gb[...]
    lg_gla = _log_sigmoid(z) * (1.0 / GLA_GATE_TEMP)
    logits = lbl[...]
    e = jnp.exp(logits - jnp.max(logits, axis=0, keepdims=True))
    sm = e / jnp.sum(e, axis=0, keepdims=True)
    lb = jnp.zeros_like(sm[0:1])
    for dd in range(1, layer + 1):
        lb = lb + sm[dd:dd + 1]
    zf = hf[...]
    ks[:, :GLA_W] = gk[...]
    ks[:, GLA_W:] = (1.0 - lb) * _sigmoid(-zf)
    a = jnp.log(lb)
    b = jnp.log1p(-lb) + _log_sigmoid(zf)
    lg_h = jnp.maximum(a, b) + jnp.log1p(jnp.exp(-jnp.abs(a - b)))
    tri_b = tri[...]
    gs[:, :GLA_W] = sum(jnp.dot(tri_b, part, preferred_element_type=F32)
                        for part in _split_bf16(lg_gla, 3))
    gs[:, GLA_W:] = sum(jnp.dot(tri_b, part, preferred_element_type=F32)
                        for part in _split_bf16(lg_h, 3))

    row = lax.broadcasted_iota(jnp.int32, (c, LANES), 0)
    ones = jnp.ones((LANES, LANES), BF16)
    o_dst = out if reverse else osum

    def chunk_body(ci, carry):
        cidx = (n_chunks - 1 - ci) if reverse else ci
        r = pl.multiple_of(cidx * c, c)
        for h in range(SCAN_HEADS):
            hl = slice(h * LANES, (h + 1) * LANES)
            q = qs[pl.ds(r, c), hl]
            k = ks[pl.ds(r, c), hl]
            g = gs[pl.ds(r, c), hl]
            if h < GLA_HEADS:
                v = gv[pl.ds(r, c), hl]
            else:
                v = hi[pl.ds(r, c), (h - GLA_HEADS) * LANES:(h - GLA_HEADS + 1) * LANES]
            g_edge = g[0:1] if reverse else g[c - 1:c]
            s_t = st[h]
            o_inter = lax.dot_general(
                (q * jnp.exp(g)).astype(BF16), s_t.astype(BF16),
                (((1,), (1,)), ((), ())), preferred_element_type=F32)
            parts = []
            for j in range(c):
                live = (row <= j) if reverse else (row >= j)
                dec = jnp.exp(jnp.where(live, g - g[j:j + 1], -jnp.inf))
                parts.append((q * dec * k[j:j + 1]).astype(BF16))
            att = jnp.dot(jnp.concatenate(parts, axis=0), ones,
                          preferred_element_type=F32)
            o_intra = att[0:c] * v[0:1]
            for j in range(1, c):
                o_intra = o_intra + att[j * c:(j + 1) * c] * v[j:j + 1]
            o_dst[pl.ds(r, c), hl] = o_inter + o_intra
            k_dec = (k * jnp.exp(g_edge - g)).astype(BF16)
            st[h] = s_t * jnp.exp(g_edge) + lax.dot_general(
                v.astype(BF16), k_dec, (((0,), (0,)), ((), ())),
                preferred_element_type=F32)
        return carry

    lax.fori_loop(0, n_chunks, chunk_body, 0)

    if not reverse:
        for h in range(SCAN_HEADS):
            hl = slice(h * LANES, (h + 1) * LANES)
            o = osum[:, hl] + ob[:, hl]
            if h < GLA_HEADS:
                gate = gg[:, hl]
            else:
                gate = hg[:, (h - GLA_HEADS) * LANES:(h - GLA_HEADS + 1) * LANES]
            y = _rms(o, onorm[:, hl]) * (gate * _sigmoid(gate))
            out[:, hl] = y.astype(out.dtype)


def _scan(u, o_b, wg, gb, lbl, tri, onorm, *, batch, seq, layer, reverse):
    tb = SCAN_BLOCK
    nt = seq // tb

    def rows(b, t):
        return b * nt + ((nt - 1 - t) if reverse else t)

    def ublock(cb):
        return pl.BlockSpec((tb, 512), lambda b, t: (rows(b, t), cb))

    def const(shape):
        return pl.BlockSpec(shape, lambda b, t: (0,) * len(shape))

    ga_spec = pl.BlockSpec((tb, LANES), lambda b, t: (rows(b, t), CB128_GA))
    wide = pl.BlockSpec((tb, SCAN_WIDTH), lambda b, t: (rows(b, t), 0))
    params = [const((LANES, GLA_W)), const((1, GLA_W)), const(lbl.shape), const((tb, tb))]
    scratch = [pltpu.VMEM((tb, SCAN_WIDTH), F32)] * 3 + [pltpu.VMEM((SCAN_HEADS, LANES, LANES), F32)]
    if reverse:
        in_specs = [ublock(CB_GQ), ublock(CB_GK), ublock(CB_GV), ga_spec,
                    ublock(CB_HQ), ublock(CB_HFB), ublock(CB_HI)] + params
        args = [u, u, u, u, u, u, u, wg, gb, lbl, tri]
        out_dtype = F32
    else:
        in_specs = [ublock(CB_GQ), ublock(CB_GK), ublock(CB_GV), ublock(CB_GG), ga_spec,
                    ublock(CB_HQ), ublock(CB_HFF), ublock(CB_HI), ublock(CB_HG), wide] + params + [
                        const((1, SCAN_WIDTH))]
        args = [u, u, u, u, u, u, u, u, u, o_b, wg, gb, lbl, tri, onorm]
        out_dtype = BF16
        scratch = scratch + [pltpu.VMEM((tb, SCAN_WIDTH), F32)]
    return pl.pallas_call(
        functools.partial(_scan_kernel, reverse, layer),
        grid=(batch, nt),
        in_specs=in_specs,
        out_specs=wide,
        out_shape=jax.ShapeDtypeStruct((batch * seq, SCAN_WIDTH), out_dtype),
        scratch_shapes=scratch,
        compiler_params=_cparams(("arbitrary", "arbitrary")),
        name="scan_bwd" if reverse else "scan_fwd",
    )(*args)


ATTN_DQK_PAD = 256
ATTN_STAB_ROW = MLA_DQK
ATTN_DV_PAD = MLA_DV + 16


def _mla_prep_kernel(cq_ref, ckv_ref, kr_ref, cs_ref, gq_ref, gkv_ref, wq_ref, wkv_ref,
                     qt_out, ka_out, vt_out):
    scale = MLA_DQK ** -0.5 * float(np.log2(np.e))
    tm = cq_ref.shape[0]
    cs = cs_ref[...]
    yq = jnp.dot(_rms(cq_ref[...], gq_ref[...]).astype(BF16), wq_ref[...],
                 preferred_element_type=F32)
    ykv = jnp.dot(_rms(ckv_ref[...], gkv_ref[...]).astype(BF16), wkv_ref[...],
                  preferred_element_type=F32)
    lane = lax.broadcasted_iota(jnp.int32, (tm, LANES), 1)
    is_rope = lane < MLA_ROPE

    def rotary(t):
        t = t * cs
        return t + pltpu.roll(t, MLA_ROPE, 1)

    k_hi = jnp.where(is_rope, rotary(kr_ref[...]),
                     (lane == MLA_ROPE).astype(F32)).astype(BF16)
    ones_row = (lax.broadcasted_iota(jnp.int32, (ATTN_DV_PAD - MLA_DV, tm), 0) == 0).astype(BF16)
    for h in range(MLA_HEADS):
        yh = yq[:, h * 256:(h + 1) * 256]
        q_hi = jnp.where(is_rope, rotary(yh[:, MLA_NOPE:]), 0.0)
        q_full = jnp.concatenate([yh[:, :MLA_NOPE], q_hi], axis=1) * scale
        qt_out[h] = q_full.T.astype(BF16)
        kvh = ykv[:, h * 256:(h + 1) * 256]
        ka_out[h, :, :MLA_NOPE] = kvh[:, :MLA_NOPE].astype(BF16)
        ka_out[h, :, MLA_NOPE:] = k_hi
        vt_out[h, :MLA_DV, :] = kvh[:, MLA_NOPE:].T.astype(BF16)
        vt_out[h, MLA_DV:, :] = ones_row


def _mla_prep(u, cs, gq, gkv, wq, wkv, *, batch, seq, tm=512):
    nt = seq // tm

    def rows(b, t):
        return b * nt + t

    def const(shape):
        return pl.BlockSpec(shape, lambda b, t: (0,) * len(shape))

    return pl.pallas_call(
        _mla_prep_kernel,
        grid=(batch, nt),
        in_specs=[
            pl.BlockSpec((tm, 512), lambda b, t: (rows(b, t), CB_CQ)),
            pl.BlockSpec((tm, 512), lambda b, t: (rows(b, t), CB_CKV)),
            pl.BlockSpec((tm, LANES), lambda b, t: (rows(b, t), CB128_KR)),
            pl.BlockSpec((tm, LANES), lambda b, t: (rows(b, t), 0)),
            const((1, MLA_Q_RANK)), const((1, MLA_KV_RANK)),
            const(wq.shape), const(wkv.shape),
        ],
        out_specs=[
            pl.BlockSpec((None, MLA_HEADS, ATTN_DQK_PAD, tm), lambda b, t: (b, 0, 0, t)),
            pl.BlockSpec((None, MLA_HEADS, tm, ATTN_DQK_PAD), lambda b, t: (b, 0, t, 0)),
            pl.BlockSpec((None, MLA_HEADS, ATTN_DV_PAD, tm), lambda b, t: (b, 0, 0, t)),
        ],
        out_shape=[
            jax.ShapeDtypeStruct((batch, MLA_HEADS, ATTN_DQK_PAD, seq), BF16),
            jax.ShapeDtypeStruct((batch, MLA_HEADS, seq, ATTN_DQK_PAD), BF16),
            jax.ShapeDtypeStruct((batch, MLA_HEADS, ATTN_DV_PAD, seq), BF16),
        ],
        compiler_params=_cparams(("parallel", "parallel")),
        name="mla_prep",
    )(u, u, u, cs, gq, gkv, wq, wkv)


ATTN_HEADROOM = 32.0
ATTN_ANCHOR_KEYS = MXU_TILE


def _attn_kernel(tk, qt_ref, ka_ref, vt_ref, o_ref, qa_ref, m_ref, acc_ref, p0_ref, p1_ref):
    tq = qt_ref.shape[1]
    n_chunks = ka_ref.shape[0] // tk
    t = MXU_TILE
    stab_rows = slice(ATTN_STAB_ROW, ATTN_STAB_ROW + 16)
    row0 = lax.broadcasted_iota(jnp.int32, (16, tq), 0) == 0

    def set_stabiliser(m):
        m_ref[...] = m
        qa_ref[stab_rows, :] = jnp.where(row0, -m, 0.0).astype(BF16)

    def bf16_round(x):
        return x.astype(BF16).astype(F32)

    def probabilities(c, p_ref):
        start = pl.multiple_of(c * tk, tk)
        mx = None
        for qh in range(tq // t):
            ql = slice(qh * t, (qh + 1) * t)
            mq = None
            for sl in range(tk // t):
                s = jnp.dot(ka_ref[pl.ds(start + sl * t, t), :], qa_ref[:, ql],
                            preferred_element_type=F32)
                cur = jnp.max(s, axis=0, keepdims=True)
                mq = cur if mq is None else jnp.maximum(mq, cur)
                p_ref[sl * t:(sl + 1) * t, ql] = jnp.exp2(s).astype(BF16)
            mx = mq if mx is None else jnp.concatenate([mx, mq], axis=1)
        return mx

    def accumulate(c, p_ref):
        start = pl.multiple_of(c * tk, tk)
        for qh in range(tq // t):
            ql = slice(qh * t, (qh + 1) * t)
            pv = None
            for sl in range(tk // t):
                d = jnp.dot(vt_ref[:, pl.ds(start + sl * t, t)], p_ref[sl * t:(sl + 1) * t, ql],
                            preferred_element_type=F32)
                pv = d if pv is None else pv + d
            acc_ref[:, ql] += pv

    def checked_probabilities(c, p_ref):
        mx = probabilities(c, p_ref)

        @pl.when(jnp.max(mx) > ATTN_HEADROOM)
        def _():
            m_old = m_ref[...]
            m_new = bf16_round(m_old + jnp.maximum(mx, 0.0))
            acc_ref[...] *= jnp.exp2(m_old - m_new)
            set_stabiliser(m_new)
            probabilities(c, p_ref)

    qa_ref[...] = qt_ref[...]
    s0 = jnp.dot(ka_ref[0:ATTN_ANCHOR_KEYS, :], qa_ref[...], preferred_element_type=F32)
    set_stabiliser(bf16_round(jnp.max(s0, axis=0, keepdims=True)))
    acc_ref[...] = jnp.zeros_like(acc_ref)

    checked_probabilities(0, p0_ref)

    def body(j, carry):
        c = 2 * j
        accumulate(c, p0_ref)
        checked_probabilities(c + 1, p1_ref)
        accumulate(c + 1, p1_ref)
        checked_probabilities(jnp.minimum(c + 2, n_chunks - 1), p0_ref)
        return carry

    lax.fori_loop(0, n_chunks // 2, body, 0)
    acc = acc_ref[...]
    o_ref[...] = (acc[:MLA_DV] / acc[MLA_DV:MLA_DV + 1]).T.astype(o_ref.dtype)


def _attention(qt, ka, vt, *, tq=1024, tk=2048):
    batch, heads, dqk, seq = qt.shape
    dvp = vt.shape[2]
    assert seq % tq == 0 and seq % (2 * tk) == 0, "key chunks are consumed in pairs"
    return pl.pallas_call(
        functools.partial(_attn_kernel, tk),
        grid=(batch, heads, seq // tq),
        in_specs=[
            pl.BlockSpec((None, None, dqk, tq), lambda b, h, i: (b, h, 0, i)),
            pl.BlockSpec((None, None, seq, dqk), lambda b, h, i: (b, h, 0, 0)),
            pl.BlockSpec((None, None, dvp, seq), lambda b, h, i: (b, h, 0, 0)),
        ],
        out_specs=pl.BlockSpec((None, tq, MLA_DV), lambda b, h, i: (b, i, h)),
        out_shape=jax.ShapeDtypeStruct((batch, seq, heads * MLA_DV), BF16),
        scratch_shapes=[pltpu.VMEM((dqk, tq), BF16), pltpu.VMEM((1, tq), F32),
                        pltpu.VMEM((dvp, tq), F32),
                        pltpu.VMEM((tk, tq), BF16), pltpu.VMEM((tk, tq), BF16)],
        compiler_params=_cparams(("parallel", "parallel", "arbitrary")),
        name="mla_attn",
    )(qt, ka, vt)


def _outproj_kernel(h_ref, ys_ref, ym_ref, w_ref, o_ref):
    ws = ys_ref.shape[1]
    o_ref[...] = (h_ref[...]
                  + jnp.dot(ys_ref[...], w_ref[:ws, :], preferred_element_type=F32)
                  + jnp.dot(ym_ref[...], w_ref[ws:, :], preferred_element_type=F32))


def _outproj(h2d, ys, ym, w, tm=512):
    m, d = h2d.shape
    return pl.pallas_call(
        _outproj_kernel,
        grid=(m // tm,),
        in_specs=[
            pl.BlockSpec((tm, d), lambda i: (i, 0)),
            pl.BlockSpec((tm, ys.shape[1]), lambda i: (i, 0)),
            pl.BlockSpec((tm, ym.shape[1]), lambda i: (i, 0)),
            pl.BlockSpec(w.shape, lambda i: (0, 0)),
        ],
        out_specs=pl.BlockSpec((tm, d), lambda i: (i, 0)),
        out_shape=jax.ShapeDtypeStruct((m, d), F32),
        compiler_params=_cparams(("parallel",)),
        name="outproj",
    )(h2d, ys, ym, w)


def _ffn_kernel(final, h_ref, g_ref, gf_ref, w1_ref, w3_ref, w2_ref, o_ref, xn_ref, acc_ref):
    j = pl.program_id(1)

    @pl.when(j == 0)
    def _():
        xn_ref[...] = _rms(h_ref[...], g_ref[...]).astype(BF16)
        acc_ref[...] = jnp.zeros_like(acc_ref)

    xn = xn_ref[...]
    a = jnp.dot(xn, w1_ref[...], preferred_element_type=F32)
    b = jnp.dot(xn, w3_ref[...], preferred_element_type=F32)
    gated = (a * _sigmoid(a) * b).astype(BF16)
    acc_ref[...] += jnp.dot(gated, w2_ref[...], preferred_element_type=F32)

    @pl.when(j == pl.num_programs(1) - 1)
    def _():
        y = h_ref[...] + acc_ref[...]
        if final:
            y = _rms(y, gf_ref[...])
        o_ref[...] = y


def _ffn(h2d, gain, gain_final, w1, w3, w2, *, final, tm=512, tf=512):
    m, d = h2d.shape
    f = w1.shape[1]
    return pl.pallas_call(
        functools.partial(_ffn_kernel, final),
        grid=(m // tm, f // tf),
        in_specs=[
            pl.BlockSpec((tm, d), lambda i, j: (i, 0)),
            pl.BlockSpec((1, d), lambda i, j: (0, 0)),
            pl.BlockSpec((1, d), lambda i, j: (0, 0)),
            pl.BlockSpec((d, tf), lambda i, j: (0, j)),
            pl.BlockSpec((d, tf), lambda i, j: (0, j)),
            pl.BlockSpec((tf, d), lambda i, j: (j, 0)),
        ],
        out_specs=pl.BlockSpec((tm, d), lambda i, j: (i, 0)),
        out_shape=jax.ShapeDtypeStruct((m, d), F32),
        scratch_shapes=[pltpu.VMEM((tm, d), BF16), pltpu.VMEM((tm, d), F32)],
        compiler_params=_cparams(("parallel", "arbitrary")),
        name="ffn",
    )(h2d, gain, gain_final, w1, w3, w2)


def _pad_heads(w, heads, dk):
    lead = w.shape[:-1]
    w = w.reshape(lead + (heads, dk))
    w = jnp.pad(w, [(0, 0)] * len(lead) + [(0, 0), (0, LANES - dk)])
    return w.reshape(lead + (heads * LANES,))


def _layout_w_in(w):
    sizes = (256, 256, 512, 16, 16, 512, 512, 512, 512, 512, 512, 512, 512, 64)
    (gq, gk, gv, gaf, gab, gg, hq, hff, hfb, hi, hg, cq, ckv, kr) = jnp.split(
        w, np.cumsum(sizes)[:-1].tolist(), axis=1)
    half = MLA_ROPE // 2
    kr_swapped = jnp.concatenate([kr[:, half:], kr[:, :half]], axis=1)
    d = w.shape[0]
    cols = [_pad_heads(gq, GLA_HEADS, GLA_DK), _pad_heads(gk, GLA_HEADS, GLA_DK), gv, gg,
            hq, hff, hfb, hi, hg, cq, ckv, kr, kr_swapped,
            gaf, gab, jnp.zeros((d, LANES - 2 * GLA_GATE_RANK), w.dtype)]
    out = jnp.concatenate(cols, axis=1)
    out = jnp.pad(out, [(0, 0), (0, U_WIDTH - out.shape[1])])
    return out.astype(BF16)


def _layout_w_qb(w):
    r = w.shape[0]
    w = w.reshape(r, MLA_HEADS, MLA_DQK)
    rope = w[:, :, MLA_NOPE:]
    half = MLA_ROPE // 2
    swapped = jnp.concatenate([rope[:, :, half:], rope[:, :, :half]], axis=2)
    return jnp.concatenate([w, swapped], axis=2).reshape(r, MLA_HEADS * 256).astype(BF16)


def _tri(tb, reverse):
    r = np.arange(tb)[:, None]
    c = np.arange(tb)[None, :]
    same = (r // SCAN_CHUNK) == (c // SCAN_CHUNK)
    keep = (c >= r) if reverse else (c <= r)
    return jnp.asarray((same & keep).astype(np.float32)).astype(BF16)


def kernel(x, positions, w_in, gla_gate_up, gla_gate_bias, gla_out_norm, hgrn_lb_logits,
           hgrn_out_norm, mla_qa_norm, mla_w_qb, mla_kva_norm, mla_w_kvb, w_out,
           norm_mix, norm_ffn, ffn_w1, ffn_w3, ffn_w2, norm_final):
    batch, seq, d = x.shape
    depth = w_in.shape[0]
    m = batch * seq

    inv_freq = 1.0 / (ROPE_THETA ** (jnp.arange(0, MLA_ROPE, 2, dtype=F32) / MLA_ROPE))
    ang = positions.astype(F32)[..., None] * inv_freq
    cos, sin = jnp.cos(ang), jnp.sin(ang)
    cs = jnp.concatenate([cos, cos, -sin, sin], axis=-1).reshape(m, LANES)

    tri_f, tri_b = _tri(SCAN_BLOCK, False), _tri(SCAN_BLOCK, True)
    h = x.reshape(m, d)
    for l in range(depth):
        u = _inproj(h, norm_mix[l].reshape(1, d), _layout_w_in(w_in[l]))

        def gate_w(direction):
            wgd = _pad_heads(gla_gate_up[l, direction], GLA_HEADS, GLA_DK)
            lo = direction * GLA_GATE_RANK
            return jnp.pad(wgd, [(lo, LANES - lo - GLA_GATE_RANK), (0, 0)])

        gb = _pad_heads(gla_gate_bias[l], GLA_HEADS, GLA_DK)
        onorm = jnp.concatenate([jnp.tile(gla_out_norm[l], GLA_HEADS),
                                 jnp.tile(hgrn_out_norm[l], HGRN_HEADS)]).reshape(1, SCAN_WIDTH)
        o_b = _scan(u, None, gate_w(1), gb[1:2], hgrn_lb_logits[1], tri_b, None,
                    batch=batch, seq=seq, layer=l, reverse=True)
        y_scan = _scan(u, o_b, gate_w(0), gb[0:1], hgrn_lb_logits[0], tri_f, onorm,
                       batch=batch, seq=seq, layer=l, reverse=False)

        q, k, vt = _mla_prep(u, cs, mla_qa_norm[l].reshape(1, -1), mla_kva_norm[l].reshape(1, -1),
                             _layout_w_qb(mla_w_qb[l]), mla_w_kvb[l].astype(BF16),
                             batch=batch, seq=seq)
        y_mla = _attention(q, k, vt).reshape(m, MLA_HEADS * MLA_DV)

        h = _outproj(h, y_scan, y_mla, w_out[l].astype(BF16))
        h = _ffn(h, norm_ffn[l].reshape(1, d), norm_final.reshape(1, d),
                 ffn_w1[l].astype(BF16), ffn_w3[l].astype(BF16), ffn_w2[l].astype(BF16),
                 final=(l == depth - 1))
    return h.reshape(batch, seq, d)
```

```python
import functools

import jax
import jax.numpy as jnp
import numpy as np
from jax import lax
from jax.experimental import pallas as pl
from jax.experimental.pallas import tpu as pltpu

D_MODEL = 2048
GLA_HEADS, GLA_DK, GLA_DV = 4, 64, 128
GLA_GATE_RANK = 16
GLA_GATE_TEMP = 16.0
HGRN_HEADS, HGRN_DK, HGRN_DV = 4, 128, 128
MLA_HEADS, MLA_Q_RANK, MLA_KV_RANK = 8, 512, 512
MLA_NOPE, MLA_ROPE, MLA_DV = 128, 64, 128
MLA_DQK = MLA_NOPE + MLA_ROPE
ROPE_THETA = 10000.0
EPS = 1e-6

LANES = 128
MXU_TILE = 256
SCAN_HEADS = GLA_HEADS + HGRN_HEADS
SCAN_WIDTH = SCAN_HEADS * LANES
GLA_W = GLA_HEADS * LANES
HGRN_W = HGRN_HEADS * LANES

CB_GQ, CB_GK, CB_GV, CB_GG, CB_HQ, CB_HFF, CB_HFB, CB_HI, CB_HG, CB_CQ, CB_CKV = range(11)
CB128_KR = 44
CB128_GA = 45
U_WIDTH = 6144

SCAN_CHUNK = 32
SCAN_BLOCK = 256
SCAN_FACTOR_LIMIT = 60.0
VMEM_LIMIT = 56 * 1024 * 1024

F32 = jnp.float32
BF16 = jnp.bfloat16


def _cparams(sem):
    return pltpu.CompilerParams(dimension_semantics=sem, vmem_limit_bytes=VMEM_LIMIT)


def _rms(x, gain):
    return x * lax.rsqrt(jnp.mean(x * x, axis=-1, keepdims=True) + EPS) * gain


def _sigmoid(z):
    return 1.0 / (1.0 + jnp.exp(-z))


def _softplus_neg_abs(z):
    t = jnp.exp(-jnp.abs(z))
    return t, jnp.log(1.0 + t)


def _split_bf16(x, n):
    parts = []
    for _ in range(n):
        p = x.astype(BF16)
        parts.append(p)
        x = x - p.astype(F32)
    return parts


def _inproj_kernel(x_ref, g_ref, w_ref, o_ref, xn_ref):
    @pl.when(pl.program_id(1) == 0)
    def _():
        xn_ref[...] = _rms(x_ref[...], g_ref[...]).astype(BF16)

    o_ref[...] = jnp.dot(xn_ref[...], w_ref[...], preferred_element_type=F32)


def _inproj(h2d, gain, w, tm=512, tn=1536):
    m, d = h2d.shape
    n = w.shape[1]
    return pl.pallas_call(
        _inproj_kernel,
        grid=(m // tm, n // tn),
        in_specs=[
            pl.BlockSpec((tm, d), lambda i, j: (i, 0)),
            pl.BlockSpec((1, d), lambda i, j: (0, 0)),
            pl.BlockSpec((d, tn), lambda i, j: (0, j)),
        ],
        out_specs=pl.BlockSpec((tm, tn), lambda i, j: (i, j)),
        out_shape=jax.ShapeDtypeStruct((m, n), F32),
        scratch_shapes=[pltpu.VMEM((tm, d), BF16)],
        compiler_params=_cparams(("parallel", "arbitrary")),
        name="inproj",
    )(h2d, gain, w)


def _scan_kernel(reverse, layer, *refs):
    if reverse:
        (gq, gk, gv, ga, hq, hf, hi, wg, gb, lbl, tri,
         out, qs, ks, gs, st) = refs
    else:
        (gq, gk, gv, gg, ga, hq, hf, hi, hg, ob, wg, gb, lbl, tri, onorm,
         out, qs, ks, gs, st, osum) = refs
    tb = qs.shape[0]
    n_chunks = tb // SCAN_CHUNK
    c = SCAN_CHUNK

    @pl.when(pl.program_id(1) == 0)
    def _():
        st[...] = jnp.zeros_like(st)

    qs[:, :GLA_W] = gq[...] * (GLA_DK ** -0.5)
    hqv = hq[...]
    qs[:, GLA_W:] = hqv * _sigmoid(hqv)
    ga_hi, ga_lo = _split_bf16(ga[...], 2)
    wg_hi, wg_lo = _split_bf16(wg[...], 2)
    z = (jnp.dot(ga_hi, wg_hi, preferred_element_type=F32)
         + jnp.dot(ga_lo, wg_hi, preferred_element_type=F32)
         + jnp.dot(ga_hi, wg_lo, preferred_element_type=F32)) + gb[...]
    lg_gla = (jnp.minimum(z, 0.0) - _softplus_neg_abs(z)[1]) * (1.0 / GLA_GATE_TEMP)
    logits = lbl[...]
    e = jnp.exp(logits - jnp.max(logits, axis=0, keepdims=True))
    sm = e / jnp.sum(e, axis=0, keepdims=True)
    lb = jnp.zeros_like(sm[0:1])
    for dd in range(1, layer + 1):
        lb = lb + sm[dd:dd + 1]
    zf = hf[...]
    t, sp = _softplus_neg_abs(zf)
    ks[:, :GLA_W] = gk[...]
    ks[:, GLA_W:] = (1.0 - lb) * (jnp.where(zf > 0.0, t, 1.0) / (1.0 + t))
    a = jnp.log(lb)
    b = jnp.log1p(-lb) + (jnp.minimum(zf, 0.0) - sp)
    lg_h = jnp.maximum(a, b) + _softplus_neg_abs(a - b)[1]
    tri_b = tri[...]
    gs[:, :GLA_W] = sum(jnp.dot(tri_b, part, preferred_element_type=F32)
                        for part in _split_bf16(lg_gla, 3))
    gs[:, GLA_W:] = sum(jnp.dot(tri_b, part, preferred_element_type=F32)
                        for part in _split_bf16(lg_h, 3))

    row = lax.broadcasted_iota(jnp.int32, (c, LANES), 0)
    ones = jnp.ones((LANES, LANES), BF16)
    ii = lax.broadcasted_iota(jnp.int32, (c, c), 0)
    jj = lax.broadcasted_iota(jnp.int32, (c, c), 1)
    live_cc = (ii <= jj) if reverse else (ii >= jj)
    o_dst = out if reverse else osum
    nt_dims = (((1,), (1,)), ((), ()))

    def intra_direct(q, k, g, v):
        parts = []
        for j in range(c):
            live = (row <= j) if reverse else (row >= j)
            dec = jnp.exp(jnp.where(live, g - g[j:j + 1], -jnp.inf))
            parts.append((q * dec * k[j:j + 1]).astype(BF16))
        att = jnp.dot(jnp.concatenate(parts, axis=0), ones,
                      preferred_element_type=F32)
        o_intra = att[0:c] * v[0:1]
        for j in range(1, c):
            o_intra = o_intra + att[j * c:(j + 1) * c] * v[j:j + 1]
        return o_intra

    tn_dims = (((0,), (0,)), ((), ()))

    def values(h, rs):
        if h < GLA_HEADS:
            return gv[rs, h * LANES:(h + 1) * LANES]
        return hi[rs, (h - GLA_HEADS) * LANES:(h - GLA_HEADS + 1) * LANES]

    def block_factored():
        order = range(n_chunks - 1, -1, -1) if reverse else range(n_chunks)
        for h in range(SCAN_HEADS):
            hl = slice(h * LANES, (h + 1) * LANES)
            staged = []
            for cidx in order:
                rs = slice(cidx * c, (cidx + 1) * c)
                q, k, g = qs[rs, hl], ks[rs, hl], gs[rs, hl]
                v = values(h, rs).astype(BF16)
                g_near = g[c - 1:c] if reverse else g[0:1]
                g_far = g[0:1] if reverse else g[c - 1:c]
                att = lax.dot_general((q * jnp.exp(g - g_near)).astype(BF16),
                                      (k * jnp.exp(g_near - g)).astype(BF16),
                                      nt_dims, preferred_element_type=F32)
                kv = lax.dot_general(v, (k * jnp.exp(g_far - g)).astype(BF16),
                                     tn_dims, preferred_element_type=F32)
                staged.append((rs, (q * jnp.exp(g)).astype(BF16), att, v, jnp.exp(g_far), kv))
            s_t = st[h]
            for rs, q_dec, att, v, decay, kv in staged:
                o_inter = lax.dot_general(q_dec, s_t.astype(BF16), nt_dims,
                                          preferred_element_type=F32)
                o_intra = jnp.dot(jnp.where(live_cc, att, 0.0).astype(BF16), v,
                                  preferred_element_type=F32)
                o_dst[rs, hl] = o_inter + o_intra
                s_t = s_t * decay + kv
            st[h] = s_t

    def chunk_direct(ci, carry):
        cidx = (n_chunks - 1 - ci) if reverse else ci
        rs = pl.ds(pl.multiple_of(cidx * c, c), c)
        for h in range(SCAN_HEADS):
            hl = slice(h * LANES, (h + 1) * LANES)
            q, k, g = qs[rs, hl], ks[rs, hl], gs[rs, hl]
            v = values(h, rs)
            g_far = g[0:1] if reverse else g[c - 1:c]
            s_t = st[h]
            o_inter = lax.dot_general((q * jnp.exp(g)).astype(BF16), s_t.astype(BF16),
                                      nt_dims, preferred_element_type=F32)
            o_dst[rs, hl] = o_inter + intra_direct(q, k, g, v)
            st[h] = s_t * jnp.exp(g_far) + lax.dot_general(
                v.astype(BF16), (k * jnp.exp(g_far - g)).astype(BF16), tn_dims,
                preferred_element_type=F32)
        return carry

    bounded = jnp.min(gs[...]) >= -SCAN_FACTOR_LIMIT
    pl.when(bounded)(block_factored)

    @pl.when(jnp.logical_not(bounded))
    def _():
        lax.fori_loop(0, n_chunks, chunk_direct, 0)

    if not reverse:
        for h in range(SCAN_HEADS):
            hl = slice(h * LANES, (h + 1) * LANES)
            o = osum[:, hl] + ob[:, hl]
            if h < GLA_HEADS:
                gate = gg[:, hl]
            else:
                gate = hg[:, (h - GLA_HEADS) * LANES:(h - GLA_HEADS + 1) * LANES]
            y = _rms(o, onorm[:, hl]) * (gate * _sigmoid(gate))
            out[:, hl] = y.astype(out.dtype)


def _scan(u, o_b, wg, gb, lbl, tri, onorm, *, batch, seq, layer, reverse):
    tb = SCAN_BLOCK
    nt = seq // tb

    def rows(b, t):
        return b * nt + ((nt - 1 - t) if reverse else t)

    def ublock(cb):
        return pl.BlockSpec((tb, 512), lambda b, t: (rows(b, t), cb))

    def const(shape):
        return pl.BlockSpec(shape, lambda b, t: (0,) * len(shape))

    ga_spec = pl.BlockSpec((tb, LANES), lambda b, t: (rows(b, t), CB128_GA))
    wide = pl.BlockSpec((tb, SCAN_WIDTH), lambda b, t: (rows(b, t), 0))
    params = [const((LANES, GLA_W)), const((1, GLA_W)), const(lbl.shape), const((tb, tb))]
    scratch = [pltpu.VMEM((tb, SCAN_WIDTH), F32)] * 3 + [pltpu.VMEM((SCAN_HEADS, LANES, LANES), F32)]
    if reverse:
        in_specs = [ublock(CB_GQ), ublock(CB_GK), ublock(CB_GV), ga_spec,
                    ublock(CB_HQ), ublock(CB_HFB), ublock(CB_HI)] + params
        args = [u, u, u, u, u, u, u, wg, gb, lbl, tri]
        out_dtype = F32
    else:
        in_specs = [ublock(CB_GQ), ublock(CB_GK), ublock(CB_GV), ublock(CB_GG), ga_spec,
                    ublock(CB_HQ), ublock(CB_HFF), ublock(CB_HI), ublock(CB_HG), wide] + params + [
                        const((1, SCAN_WIDTH))]
        args = [u, u, u, u, u, u, u, u, u, o_b, wg, gb, lbl, tri, onorm]
        out_dtype = BF16
        scratch = scratch + [pltpu.VMEM((tb, SCAN_WIDTH), F32)]
    return pl.pallas_call(
        functools.partial(_scan_kernel, reverse, layer),
        grid=(batch, nt),
        in_specs=in_specs,
        out_specs=wide,
        out_shape=jax.ShapeDtypeStruct((batch * seq, SCAN_WIDTH), out_dtype),
        scratch_shapes=scratch,
        compiler_params=_cparams(("arbitrary", "arbitrary")),
        name="scan_bwd" if reverse else "scan_fwd",
    )(*args)


ATTN_DQK_PAD = 256
ATTN_STAB_ROW = MLA_DQK
ATTN_DV_PAD = MLA_DV + 16


def _mla_prep_kernel(cq_ref, ckv_ref, kr_ref, cs_ref, gq_ref, gkv_ref, wq_ref, wkv_ref,
                     qt_out, ka_out, vt_out):
    scale = MLA_DQK ** -0.5 * float(np.log2(np.e))
    tm = cq_ref.shape[0]
    cs = cs_ref[...]
    yq = jnp.dot(_rms(cq_ref[...], gq_ref[...]).astype(BF16), wq_ref[...],
                 preferred_element_type=F32)
    ykv = jnp.dot(_rms(ckv_ref[...], gkv_ref[...]).astype(BF16), wkv_ref[...],
                  preferred_element_type=F32)
    lane = lax.broadcasted_iota(jnp.int32, (tm, LANES), 1)
    is_rope = lane < MLA_ROPE

    def rotary(t):
        t = t * cs
        return t + pltpu.roll(t, MLA_ROPE, 1)

    k_hi = jnp.where(is_rope, rotary(kr_ref[...]),
                     (lane == MLA_ROPE).astype(F32)).astype(BF16)
    ones_row = (lax.broadcasted_iota(jnp.int32, (ATTN_DV_PAD - MLA_DV, tm), 0) == 0).astype(BF16)
    for h in range(MLA_HEADS):
        yh = yq[:, h * 256:(h + 1) * 256]
        q_hi = jnp.where(is_rope, rotary(yh[:, MLA_NOPE:]), 0.0)
        q_full = jnp.concatenate([yh[:, :MLA_NOPE], q_hi], axis=1) * scale
        qt_out[h] = q_full.T.astype(BF16)
        kvh = ykv[:, h * 256:(h + 1) * 256]
        ka_out[h, :, :MLA_NOPE] = kvh[:, :MLA_NOPE].astype(BF16)
        ka_out[h, :, MLA_NOPE:] = k_hi
        vt_out[h, :MLA_DV, :] = kvh[:, MLA_NOPE:].T.astype(BF16)
        vt_out[h, MLA_DV:, :] = ones_row


def _mla_prep(u, cs, gq, gkv, wq, wkv, *, batch, seq, tm=512):
    nt = seq // tm

    def rows(b, t):
        return b * nt + t

    def const(shape):
        return pl.BlockSpec(shape, lambda b, t: (0,) * len(shape))

    return pl.pallas_call(
        _mla_prep_kernel,
        grid=(batch, nt),
        in_specs=[
            pl.BlockSpec((tm, 512), lambda b, t: (rows(b, t), CB_CQ)),
            pl.BlockSpec((tm, 512), lambda b, t: (rows(b, t), CB_CKV)),
            pl.BlockSpec((tm, LANES), lambda b, t: (rows(b, t), CB128_KR)),
            pl.BlockSpec((tm, LANES), lambda b, t: (rows(b, t), 0)),
            const((1, MLA_Q_RANK)), const((1, MLA_KV_RANK)),
            const(wq.shape), const(wkv.shape),
        ],
        out_specs=[
            pl.BlockSpec((None, MLA_HEADS, ATTN_DQK_PAD, tm), lambda b, t: (b, 0, 0, t)),
            pl.BlockSpec((None, MLA_HEADS, tm, ATTN_DQK_PAD), lambda b, t: (b, 0, t, 0)),
            pl.BlockSpec((None, MLA_HEADS, ATTN_DV_PAD, tm), lambda b, t: (b, 0, 0, t)),
        ],
        out_shape=[
            jax.ShapeDtypeStruct((batch, MLA_HEADS, ATTN_DQK_PAD, seq), BF16),
            jax.ShapeDtypeStruct((batch, MLA_HEADS, seq, ATTN_DQK_PAD), BF16),
            jax.ShapeDtypeStruct((batch, MLA_HEADS, ATTN_DV_PAD, seq), BF16),
        ],
        compiler_params=_cparams(("parallel", "parallel")),
        name="mla_prep",
    )(u, u, u, cs, gq, gkv, wq, wkv)


ATTN_HEADROOM = 32.0
ATTN_ANCHOR_KEYS = MXU_TILE


def _attn_kernel(tk, qt_ref, ka_ref, vt_ref, o_ref, qa_ref, m_ref, acc_ref, p0_ref, p1_ref):
    tq = qt_ref.shape[1]
    n_chunks = ka_ref.shape[0] // tk
    t = MXU_TILE
    stab_rows = slice(ATTN_STAB_ROW, ATTN_STAB_ROW + 16)
    row0 = lax.broadcasted_iota(jnp.int32, (16, tq), 0) == 0

    def set_stabiliser(m):
        m_ref[...] = m
        qa_ref[stab_rows, :] = jnp.where(row0, -m, 0.0).astype(BF16)

    def bf16_round(x):
        return x.astype(BF16).astype(F32)

    def probabilities(c, p_ref):
        start = pl.multiple_of(c * tk, tk)
        mx = None
        for qh in range(tq // t):
            ql = slice(qh * t, (qh + 1) * t)
            mq = None
            for sl in range(tk // t):
                s = jnp.dot(ka_ref[pl.ds(start + sl * t, t), :], qa_ref[:, ql],
                            preferred_element_type=F32)
                cur = jnp.max(s, axis=0, keepdims=True)
                mq = cur if mq is None else jnp.maximum(mq, cur)
                p_ref[sl * t:(sl + 1) * t, ql] = jnp.exp2(s).astype(BF16)
            mx = mq if mx is None else jnp.concatenate([mx, mq], axis=1)
        return mx

    def accumulate(c, p_ref):
        start = pl.multiple_of(c * tk, tk)
        for qh in range(tq // t):
            ql = slice(qh * t, (qh + 1) * t)
            pv = None
            for sl in range(tk // t):
                d = jnp.dot(vt_ref[:, pl.ds(start + sl * t, t)], p_ref[sl * t:(sl + 1) * t, ql],
                            preferred_element_type=F32)
                pv = d if pv is None else pv + d
            acc_ref[:, ql] += pv

    def checked_probabilities(c, p_ref):
        mx = probabilities(c, p_ref)

        @pl.when(jnp.max(mx) > ATTN_HEADROOM)
        def _():
            m_old = m_ref[...]
            m_new = bf16_round(m_old + jnp.maximum(mx, 0.0))
            acc_ref[...] *= jnp.exp2(m_old - m_new)
            set_stabiliser(m_new)
            probabilities(c, p_ref)

    qa_ref[...] = qt_ref[...]
    s0 = jnp.dot(ka_ref[0:ATTN_ANCHOR_KEYS, :], qa_ref[...], preferred_element_type=F32)
    set_stabiliser(bf16_round(jnp.max(s0, axis=0, keepdims=True)))
    acc_ref[...] = jnp.zeros_like(acc_ref)

    checked_probabilities(0, p0_ref)

    def body(j, carry):
        c = 2 * j
        accumulate(c, p0_ref)
        checked_probabilities(c + 1, p1_ref)
        accumulate(c + 1, p1_ref)
        checked_probabilities(jnp.minimum(c + 2, n_chunks - 1), p0_ref)
        return carry

    lax.fori_loop(0, n_chunks // 2, body, 0)
    acc = acc_ref[...]
    o_ref[...] = (acc[:MLA_DV] / acc[MLA_DV:MLA_DV + 1]).T.astype(o_ref.dtype)


def _attention(qt, ka, vt, *, tq=2048, tk=1024):
    batch, heads, dqk, seq = qt.shape
    dvp = vt.shape[2]
    assert seq % tq == 0 and seq % (2 * tk) == 0, "key chunks are consumed in pairs"
    return pl.pallas_call(
        functools.partial(_attn_kernel, tk),
        grid=(batch, heads, seq // tq),
        in_specs=[
            pl.BlockSpec((None, None, dqk, tq), lambda b, h, i: (b, h, 0, i)),
            pl.BlockSpec((None, None, seq, dqk), lambda b, h, i: (b, h, 0, 0)),
            pl.BlockSpec((None, None, dvp, seq), lambda b, h, i: (b, h, 0, 0)),
        ],
        out_specs=pl.BlockSpec((None, tq, MLA_DV), lambda b, h, i: (b, i, h)),
        out_shape=jax.ShapeDtypeStruct((batch, seq, heads * MLA_DV), BF16),
        scratch_shapes=[pltpu.VMEM((dqk, tq), BF16), pltpu.VMEM((1, tq), F32),
                        pltpu.VMEM((dvp, tq), F32),
                        pltpu.VMEM((tk, tq), BF16), pltpu.VMEM((tk, tq), BF16)],
        compiler_params=_cparams(("parallel", "parallel", "arbitrary")),
        name="mla_attn",
    )(qt, ka, vt)


def _outproj_kernel(h_ref, ys_ref, ym_ref, w_ref, o_ref):
    ws = ys_ref.shape[1]
    o_ref[...] = (h_ref[...]
                  + jnp.dot(ys_ref[...], w_ref[:ws, :], preferred_element_type=F32)
                  + jnp.dot(ym_ref[...], w_ref[ws:, :], preferred_element_type=F32))


def _outproj(h2d, ys, ym, w, tm=512):
    m, d = h2d.shape
    return pl.pallas_call(
        _outproj_kernel,
        grid=(m // tm,),
        in_specs=[
            pl.BlockSpec((tm, d), lambda i: (i, 0)),
            pl.BlockSpec((tm, ys.shape[1]), lambda i: (i, 0)),
            pl.BlockSpec((tm, ym.shape[1]), lambda i: (i, 0)),
            pl.BlockSpec(w.shape, lambda i: (0, 0)),
        ],
        out_specs=pl.BlockSpec((tm, d), lambda i: (i, 0)),
        out_shape=jax.ShapeDtypeStruct((m, d), F32),
        compiler_params=_cparams(("parallel",)),
        name="outproj",
    )(h2d, ys, ym, w)


def _ffn_kernel(final, h_ref, g_ref, gf_ref, w1_ref, w3_ref, w2_ref, o_ref, xn_ref, acc_ref):
    j = pl.program_id(1)

    @pl.when(j == 0)
    def _():
        xn_ref[...] = _rms(h_ref[...], g_ref[...]).astype(BF16)
        acc_ref[...] = jnp.zeros_like(acc_ref)

    xn = xn_ref[...]
    a = jnp.dot(xn, w1_ref[...], preferred_element_type=F32)
    b = jnp.dot(xn, w3_ref[...], preferred_element_type=F32)
    gated = (a * _sigmoid(a) * b).astype(BF16)
    acc_ref[...] += jnp.dot(gated, w2_ref[...], preferred_element_type=F32)

    @pl.when(j == pl.num_programs(1) - 1)
    def _():
        y = h_ref[...] + acc_ref[...]
        if final:
            y = _rms(y, gf_ref[...])
        o_ref[...] = y


def _ffn(h2d, gain, gain_final, w1, w3, w2, *, final, tm=512, tf=512):
    m, d = h2d.shape
    f = w1.shape[1]
    return pl.pallas_call(
        functools.partial(_ffn_kernel, final),
        grid=(m // tm, f // tf),
        in_specs=[
            pl.BlockSpec((tm, d), lambda i, j: (i, 0)),
            pl.BlockSpec((1, d), lambda i, j: (0, 0)),
            pl.BlockSpec((1, d), lambda i, j: (0, 0)),
            pl.BlockSpec((d, tf), lambda i, j: (0, j)),
            pl.BlockSpec((d, tf), lambda i, j: (0, j)),
            pl.BlockSpec((tf, d), lambda i, j: (j, 0)),
        ],
        out_specs=pl.BlockSpec((tm, d), lambda i, j: (i, 0)),
        out_shape=jax.ShapeDtypeStruct((m, d), F32),
        scratch_shapes=[pltpu.VMEM((tm, d), BF16), pltpu.VMEM((tm, d), F32)],
        compiler_params=_cparams(("parallel", "arbitrary")),
        name="ffn",
    )(h2d, gain, gain_final, w1, w3, w2)


def _pad_heads(w, heads, dk):
    lead = w.shape[:-1]
    w = w.reshape(lead + (heads, dk))
    w = jnp.pad(w, [(0, 0)] * len(lead) + [(0, 0), (0, LANES - dk)])
    return w.reshape(lead + (heads * LANES,))


def _layout_w_in(w):
    sizes = (256, 256, 512, 16, 16, 512, 512, 512, 512, 512, 512, 512, 512, 64)
    (gq, gk, gv, gaf, gab, gg, hq, hff, hfb, hi, hg, cq, ckv, kr) = jnp.split(
        w, np.cumsum(sizes)[:-1].tolist(), axis=1)
    half = MLA_ROPE // 2
    kr_swapped = jnp.concatenate([kr[:, half:], kr[:, :half]], axis=1)
    d = w.shape[0]
    cols = [_pad_heads(gq, GLA_HEADS, GLA_DK), _pad_heads(gk, GLA_HEADS, GLA_DK), gv, gg,
            hq, hff, hfb, hi, hg, cq, ckv, kr, kr_swapped,
            gaf, gab, jnp.zeros((d, LANES - 2 * GLA_GATE_RANK), w.dtype)]
    out = jnp.concatenate(cols, axis=1)
    out = jnp.pad(out, [(0, 0), (0, U_WIDTH - out.shape[1])])
    return out.astype(BF16)


def _layout_w_qb(w):
    r = w.shape[0]
    w = w.reshape(r, MLA_HEADS, MLA_DQK)
    rope = w[:, :, MLA_NOPE:]
    half = MLA_ROPE // 2
    swapped = jnp.concatenate([rope[:, :, half:], rope[:, :, :half]], axis=2)
    return jnp.concatenate([w, swapped], axis=2).reshape(r, MLA_HEADS * 256).astype(BF16)


def _tri(tb, reverse):
    r = np.arange(tb)[:, None]
    c = np.arange(tb)[None, :]
    same = (r // SCAN_CHUNK) == (c // SCAN_CHUNK)
    keep = (c >= r) if reverse else (c <= r)
    return jnp.asarray((same & keep).astype(np.float32)).astype(BF16)


def kernel(x, positions, w_in, gla_gate_up, gla_gate_bias, gla_out_norm, hgrn_lb_logits,
           hgrn_out_norm, mla_qa_norm, mla_w_qb, mla_kva_norm, mla_w_kvb, w_out,
           norm_mix, norm_ffn, ffn_w1, ffn_w3, ffn_w2, norm_final):
    batch, seq, d = x.shape
    depth = w_in.shape[0]
    m = batch * seq

    inv_freq = 1.0 / (ROPE_THETA ** (jnp.arange(0, MLA_ROPE, 2, dtype=F32) / MLA_ROPE))
    ang = positions.astype(F32)[..., None] * inv_freq
    cos, sin = jnp.cos(ang), jnp.sin(ang)
    cs = jnp.concatenate([cos, cos, -sin, sin], axis=-1).reshape(m, LANES)

    tri_f, tri_b = _tri(SCAN_BLOCK, False), _tri(SCAN_BLOCK, True)
    h = x.reshape(m, d)
    for l in range(depth):
        u = _inproj(h, norm_mix[l].reshape(1, d), _layout_w_in(w_in[l]))

        def gate_w(direction):
            wgd = _pad_heads(gla_gate_up[l, direction], GLA_HEADS, GLA_DK)
            lo = direction * GLA_GATE_RANK
            return jnp.pad(wgd, [(lo, LANES - lo - GLA_GATE_RANK), (0, 0)])

        gb = _pad_heads(gla_gate_bias[l], GLA_HEADS, GLA_DK)
        onorm = jnp.concatenate([jnp.tile(gla_out_norm[l], GLA_HEADS),
                                 jnp.tile(hgrn_out_norm[l], HGRN_HEADS)]).reshape(1, SCAN_WIDTH)
        o_b = _scan(u, None, gate_w(1), gb[1:2], hgrn_lb_logits[1], tri_b, None,
                    batch=batch, seq=seq, layer=l, reverse=True)
        y_scan = _scan(u, o_b, gate_w(0), gb[0:1], hgrn_lb_logits[0], tri_f, onorm,
                       batch=batch, seq=seq, layer=l, reverse=False)

        q, k, vt = _mla_prep(u, cs, mla_qa_norm[l].reshape(1, -1), mla_kva_norm[l].reshape(1, -1),
                             _layout_w_qb(mla_w_qb[l]), mla_w_kvb[l].astype(BF16),
                             batch=batch, seq=seq)
        y_mla = _attention(q, k, vt).reshape(m, MLA_HEADS * MLA_DV)

        h = _outproj(h, y_scan, y_mla, w_out[l].astype(BF16))
        h = _ffn(h, norm_ffn[l].reshape(1, d), norm_final.reshape(1, d),
                 ffn_w1[l].astype(BF16), ffn_w3[l].astype(BF16), ffn_w2[l].astype(BF16),
                 final=(l == depth - 1))
    return h.reshape(batch, seq, d)
```

```python
import functools

import jax
import jax.numpy as jnp
import numpy as np
from jax import lax
from jax.experimental import pallas as pl
from jax.experimental.pallas import tpu as pltpu

D_MODEL = 2048
GLA_HEADS, GLA_DK, GLA_DV = 4, 64, 128
GLA_GATE_RANK = 16
GLA_GATE_TEMP = 16.0
HGRN_HEADS, HGRN_DK, HGRN_DV = 4, 128, 128
MLA_HEADS, MLA_Q_RANK, MLA_KV_RANK = 8, 512, 512
MLA_NOPE, MLA_ROPE, MLA_DV = 128, 64, 128
MLA_DQK = MLA_NOPE + MLA_ROPE
ROPE_THETA = 10000.0
EPS = 1e-6

LANES = 128
SUBLANES = 8
MXU_TILE = 256
SCAN_HEADS = GLA_HEADS + HGRN_HEADS
SCAN_WIDTH = SCAN_HEADS * LANES
GLA_W = GLA_HEADS * LANES
HGRN_W = HGRN_HEADS * LANES

CB_GQ, CB_GK, CB_GV, CB_GG, CB_HQ, CB_HFF, CB_HFB, CB_HI, CB_HG, CB_CQ, CB_CKV = range(11)
CB128_KR = 44
CB128_GA = 45
U_WIDTH = 6144

SCAN_CHUNK = 32
SCAN_BLOCK = 256
SCAN_FACTOR_LIMIT = 60.0
VMEM_LIMIT = 56 * 1024 * 1024

F32 = jnp.float32
BF16 = jnp.bfloat16


def _cparams(sem):
    return pltpu.CompilerParams(dimension_semantics=sem, vmem_limit_bytes=VMEM_LIMIT)


def _rms(x, gain):
    return x * lax.rsqrt(jnp.mean(x * x, axis=-1, keepdims=True) + EPS) * gain


def _sigmoid(z):
    return 1.0 / (1.0 + jnp.exp(-z))


def _softplus_neg_abs(z):
    t = jnp.exp(-jnp.abs(z))
    return t, jnp.log(1.0 + t)


def _split_bf16(x, n):
    parts = []
    for _ in range(n):
        p = x.astype(BF16)
        parts.append(p)
        x = x - p.astype(F32)
    return parts


def _inproj_kernel(x_ref, g_ref, w_ref, o_ref, xn_ref):
    @pl.when(pl.program_id(1) == 0)
    def _():
        xn_ref[...] = _rms(x_ref[...], g_ref[...]).astype(BF16)

    tn = o_ref.shape[1]
    cols = pl.ds(pl.multiple_of(pl.program_id(1) * tn, tn), tn)
    o_ref[...] = jnp.dot(xn_ref[...], w_ref[:, cols], preferred_element_type=F32)


def _inproj(h2d, gain, w, tm=512, tn=1536):
    m, d = h2d.shape
    n = w.shape[1]
    return pl.pallas_call(
        _inproj_kernel,
        grid=(m // tm, n // tn),
        in_specs=[
            pl.BlockSpec((tm, d), lambda i, j: (i, 0)),
            pl.BlockSpec((1, d), lambda i, j: (0, 0)),
            pl.BlockSpec((d, n), lambda i, j: (0, 0), pipeline_mode=pl.Buffered(1)),
        ],
        out_specs=pl.BlockSpec((tm, tn), lambda i, j: (i, j)),
        out_shape=jax.ShapeDtypeStruct((m, n), F32),
        scratch_shapes=[pltpu.VMEM((tm, d), BF16)],
        compiler_params=_cparams(("parallel", "arbitrary")),
        name="inproj",
    )(h2d, gain, w)


def _scan_kernel(reverse, layer, *refs):
    if reverse:
        (gq, gk, gv, ga, hq, hf, hi, wg, gb, lbl, tri,
         out, qs, ks, gs, st) = refs
    else:
        (gq, gk, gv, gg, ga, hq, hf, hi, hg, ob, wg, gb, lbl, tri, onorm,
         out, qs, ks, gs, st, osum) = refs
    tb = qs.shape[0]
    n_chunks = tb // SCAN_CHUNK
    c = SCAN_CHUNK

    @pl.when(pl.program_id(1) == 0)
    def _():
        st[...] = jnp.zeros_like(st)

    qs[:, :GLA_W] = gq[...] * (GLA_DK ** -0.5)
    hqv = hq[...]
    qs[:, GLA_W:] = hqv * _sigmoid(hqv)
    ga_hi, ga_lo = _split_bf16(ga[...], 2)
    wg_hi, wg_lo = _split_bf16(wg[...], 2)
    z = (jnp.dot(ga_hi, wg_hi, preferred_element_type=F32)
         + jnp.dot(ga_lo, wg_hi, preferred_element_type=F32)
         + jnp.dot(ga_hi, wg_lo, preferred_element_type=F32)) + gb[...]
    lg_gla = (jnp.minimum(z, 0.0) - _softplus_neg_abs(z)[1]) * (1.0 / GLA_GATE_TEMP)
    logits = lbl[...]
    e = jnp.exp(logits - jnp.max(logits, axis=0, keepdims=True))
    sm = e / jnp.sum(e, axis=0, keepdims=True)
    lb = jnp.zeros_like(sm[0:1])
    for dd in range(1, layer + 1):
        lb = lb + sm[dd:dd + 1]
    zf = hf[...]
    t, sp = _softplus_neg_abs(zf)
    ks[:, :GLA_W] = gk[...]
    ks[:, GLA_W:] = (1.0 - lb) * (jnp.where(zf > 0.0, t, 1.0) / (1.0 + t))
    a = jnp.log(lb)
    b = jnp.log1p(-lb) + (jnp.minimum(zf, 0.0) - sp)
    lg_h = jnp.maximum(a, b) + _softplus_neg_abs(a - b)[1]
    tri_b = tri[...]
    gs[:, :GLA_W] = sum(jnp.dot(tri_b, part, preferred_element_type=F32)
                        for part in _split_bf16(lg_gla, 3))
    gs[:, GLA_W:] = sum(jnp.dot(tri_b, part, preferred_element_type=F32)
                        for part in _split_bf16(lg_h, 3))

    row = lax.broadcasted_iota(jnp.int32, (c, LANES), 0)
    ones = jnp.ones((LANES, LANES), BF16)
    ii = lax.broadcasted_iota(jnp.int32, (c, c), 0)
    jj = lax.broadcasted_iota(jnp.int32, (c, c), 1)
    live_cc = (ii <= jj) if reverse else (ii >= jj)
    o_dst = out if reverse else osum
    nt_dims = (((1,), (1,)), ((), ()))

    def intra_direct(q, k, g, v):
        parts = []
        for j in range(c):
            live = (row <= j) if reverse else (row >= j)
            dec = jnp.exp(jnp.where(live, g - g[j:j + 1], -jnp.inf))
            parts.append((q * dec * k[j:j + 1]).astype(BF16))
        att = jnp.dot(jnp.concatenate(parts, axis=0), ones,
                      preferred_element_type=F32)
        o_intra = att[0:c] * v[0:1]
        for j in range(1, c):
            o_intra = o_intra + att[j * c:(j + 1) * c] * v[j:j + 1]
        return o_intra

    tn_dims = (((0,), (0,)), ((), ()))

    def values(h, rs):
        if h < GLA_HEADS:
            return gv[rs, h * LANES:(h + 1) * LANES]
        return hi[rs, (h - GLA_HEADS) * LANES:(h - GLA_HEADS + 1) * LANES]

    def block_factored():
        order = range(n_chunks - 1, -1, -1) if reverse else range(n_chunks)
        for h in range(SCAN_HEADS):
            hl = slice(h * LANES, (h + 1) * LANES)
            staged = []
            for cidx in order:
                rs = slice(cidx * c, (cidx + 1) * c)
                q, k, g = qs[rs, hl], ks[rs, hl], gs[rs, hl]
                v = values(h, rs).astype(BF16)
                g_near = g[c - 1:c] if reverse else g[0:1]
                g_far = g[0:1] if reverse else g[c - 1:c]
                att = lax.dot_general((q * jnp.exp(g - g_near)).astype(BF16),
                                      (k * jnp.exp(g_near - g)).astype(BF16),
                                      nt_dims, preferred_element_type=F32)
                kv = lax.dot_general(v, (k * jnp.exp(g_far - g)).astype(BF16),
                                     tn_dims, preferred_element_type=F32)
                staged.append((rs, (q * jnp.exp(g)).astype(BF16), att, v, jnp.exp(g_far), kv))
            s_t = st[h]
            for rs, q_dec, att, v, decay, kv in staged:
                o_inter = lax.dot_general(q_dec, s_t.astype(BF16), nt_dims,
                                          preferred_element_type=F32)
                o_intra = jnp.dot(jnp.where(live_cc, att, 0.0).astype(BF16), v,
                                  preferred_element_type=F32)
                o_dst[rs, hl] = o_inter + o_intra
                s_t = s_t * decay + kv
            st[h] = s_t

    def chunk_direct(ci, carry):
        cidx = (n_chunks - 1 - ci) if reverse else ci
        rs = pl.ds(pl.multiple_of(cidx * c, c), c)
        for h in range(SCAN_HEADS):
            hl = slice(h * LANES, (h + 1) * LANES)
            q, k, g = qs[rs, hl], ks[rs, hl], gs[rs, hl]
            v = values(h, rs)
            g_far = g[0:1] if reverse else g[c - 1:c]
            s_t = st[h]
            o_inter = lax.dot_general((q * jnp.exp(g)).astype(BF16), s_t.astype(BF16),
                                      nt_dims, preferred_element_type=F32)
            o_dst[rs, hl] = o_inter + intra_direct(q, k, g, v)
            st[h] = s_t * jnp.exp(g_far) + lax.dot_general(
                v.astype(BF16), (k * jnp.exp(g_far - g)).astype(BF16), tn_dims,
                preferred_element_type=F32)
        return carry

    bounded = jnp.min(gs[...]) >= -SCAN_FACTOR_LIMIT
    pl.when(bounded)(block_factored)

    @pl.when(jnp.logical_not(bounded))
    def _():
        lax.fori_loop(0, n_chunks, chunk_direct, 0)

    if not reverse:
        for h in range(SCAN_HEADS):
            hl = slice(h * LANES, (h + 1) * LANES)
            o = osum[:, hl] + ob[:, hl]
            if h < GLA_HEADS:
                gate = gg[:, hl]
            else:
                gate = hg[:, (h - GLA_HEADS) * LANES:(h - GLA_HEADS + 1) * LANES]
            y = _rms(o, onorm[:, hl]) * (gate * _sigmoid(gate))
            out[:, hl] = y.astype(out.dtype)


def _scan(u, o_b, wg, gb, lbl, tri, onorm, *, batch, seq, layer, reverse):
    tb = SCAN_BLOCK
    nt = seq // tb

    def rows(b, t):
        return b * nt + ((nt - 1 - t) if reverse else t)

    def ublock(cb):
        return pl.BlockSpec((tb, 512), lambda b, t: (rows(b, t), cb))

    def const(shape):
        return pl.BlockSpec(shape, lambda b, t: (0,) * len(shape))

    ga_spec = pl.BlockSpec((tb, LANES), lambda b, t: (rows(b, t), CB128_GA))
    wide = pl.BlockSpec((tb, SCAN_WIDTH), lambda b, t: (rows(b, t), 0))
    params = [const((LANES, GLA_W)), const((1, GLA_W)), const(lbl.shape), const((tb, tb))]
    scratch = [pltpu.VMEM((tb, SCAN_WIDTH), F32)] * 3 + [pltpu.VMEM((SCAN_HEADS, LANES, LANES), F32)]
    if reverse:
        in_specs = [ublock(CB_GQ), ublock(CB_GK), ublock(CB_GV), ga_spec,
                    ublock(CB_HQ), ublock(CB_HFB), ublock(CB_HI)] + params
        args = [u, u, u, u, u, u, u, wg, gb, lbl, tri]
        out_dtype = F32
    else:
        in_specs = [ublock(CB_GQ), ublock(CB_GK), ublock(CB_GV), ublock(CB_GG), ga_spec,
                    ublock(CB_HQ), ublock(CB_HFF), ublock(CB_HI), ublock(CB_HG), wide] + params + [
                        const((1, SCAN_WIDTH))]
        args = [u, u, u, u, u, u, u, u, u, o_b, wg, gb, lbl, tri, onorm]
        out_dtype = BF16
        scratch = scratch + [pltpu.VMEM((tb, SCAN_WIDTH), F32)]
    return pl.pallas_call(
        functools.partial(_scan_kernel, reverse, layer),
        grid=(batch, nt),
        in_specs=in_specs,
        out_specs=wide,
        out_shape=jax.ShapeDtypeStruct((batch * seq, SCAN_WIDTH), out_dtype),
        scratch_shapes=scratch,
        compiler_params=_cparams(("arbitrary", "arbitrary")),
        name="scan_bwd" if reverse else "scan_fwd",
    )(*args)


ATTN_DQK_PAD = 256
ATTN_STAB_ROW = MLA_DQK


def _mla_prep_kernel(cq_ref, ckv_ref, kr_ref, cs_ref, gq_ref, gkv_ref, wq_ref, wkv_ref,
                     qt_out, ka_out, vt_out):
    scale = MLA_DQK ** -0.5 * float(np.log2(np.e))
    tm = cq_ref.shape[0]
    cs = cs_ref[...]
    yq = jnp.dot(_rms(cq_ref[...], gq_ref[...]).astype(BF16), wq_ref[...],
                 preferred_element_type=F32)
    ykv = jnp.dot(_rms(ckv_ref[...], gkv_ref[...]).astype(BF16), wkv_ref[...],
                  preferred_element_type=F32)
    lane = lax.broadcasted_iota(jnp.int32, (tm, LANES), 1)
    is_rope = lane < MLA_ROPE

    def rotary(t):
        t = t * cs
        return t + pltpu.roll(t, MLA_ROPE, 1)

    k_hi = jnp.where(is_rope, rotary(kr_ref[...]),
                     (lane == MLA_ROPE).astype(F32)).astype(BF16)
    for h in range(MLA_HEADS):
        yh = yq[:, h * 256:(h + 1) * 256]
        q_hi = jnp.where(is_rope, rotary(yh[:, MLA_NOPE:]), 0.0)
        q_full = jnp.concatenate([yh[:, :MLA_NOPE], q_hi], axis=1) * scale
        qt_out[h] = q_full.T.astype(BF16)
        kvh = ykv[:, h * 256:(h + 1) * 256]
        ka_out[h, :, :MLA_NOPE] = kvh[:, :MLA_NOPE].astype(BF16)
        ka_out[h, :, MLA_NOPE:] = k_hi
        vt_out[h] = kvh[:, MLA_NOPE:].T.astype(BF16)


def _mla_prep(u, cs, gq, gkv, wq, wkv, *, batch, seq, tm=512):
    nt = seq // tm

    def rows(b, t):
        return b * nt + t

    def const(shape):
        return pl.BlockSpec(shape, lambda b, t: (0,) * len(shape))

    return pl.pallas_call(
        _mla_prep_kernel,
        grid=(batch, nt),
        in_specs=[
            pl.BlockSpec((tm, 512), lambda b, t: (rows(b, t), CB_CQ)),
            pl.BlockSpec((tm, 512), lambda b, t: (rows(b, t), CB_CKV)),
            pl.BlockSpec((tm, LANES), lambda b, t: (rows(b, t), CB128_KR)),
            pl.BlockSpec((tm, LANES), lambda b, t: (rows(b, t), 0)),
            const((1, MLA_Q_RANK)), const((1, MLA_KV_RANK)),
            const(wq.shape), const(wkv.shape),
        ],
        out_specs=[
            pl.BlockSpec((None, MLA_HEADS, ATTN_DQK_PAD, tm), lambda b, t: (b, 0, 0, t)),
            pl.BlockSpec((None, MLA_HEADS, tm, ATTN_DQK_PAD), lambda b, t: (b, 0, t, 0)),
            pl.BlockSpec((None, MLA_HEADS, MLA_DV, tm), lambda b, t: (b, 0, 0, t)),
        ],
        out_shape=[
            jax.ShapeDtypeStruct((batch, MLA_HEADS, ATTN_DQK_PAD, seq), BF16),
            jax.ShapeDtypeStruct((batch, MLA_HEADS, seq, ATTN_DQK_PAD), BF16),
            jax.ShapeDtypeStruct((batch, MLA_HEADS, MLA_DV, seq), BF16),
        ],
        compiler_params=_cparams(("parallel", "parallel")),
        name="mla_prep",
    )(u, u, u, cs, gq, gkv, wq, wkv)


ATTN_HEADROOM = 32.0
ATTN_ANCHOR_KEYS = MXU_TILE


def _attn_kernel(tk, qt_ref, ka_ref, vt_ref, o_ref, qa_ref, m_ref, acc_ref, l_ref,
                 p0_ref, p1_ref, ls0_ref, ls1_ref):
    tq = qt_ref.shape[1]
    n_chunks = ka_ref.shape[0] // tk
    t = MXU_TILE
    sub = l_ref.shape[0]
    stab_rows = slice(ATTN_STAB_ROW, ATTN_STAB_ROW + 16)
    row0 = lax.broadcasted_iota(jnp.int32, (16, tq), 0) == 0

    def set_stabiliser(m):
        m_ref[...] = m
        qa_ref[stab_rows, :] = jnp.where(row0, -m, 0.0).astype(BF16)

    def bf16_round(x):
        return x.astype(BF16).astype(F32)

    def probabilities(c, p_ref, ls_ref):
        start = pl.multiple_of(c * tk, tk)
        mx = None
        for qh in range(tq // t):
            ql = slice(qh * t, (qh + 1) * t)
            mq, ls = None, None
            for sl in range(tk // t):
                s = jnp.dot(ka_ref[pl.ds(start + sl * t, t), :], qa_ref[:, ql],
                            preferred_element_type=F32)
                cur = jnp.max(s, axis=0, keepdims=True)
                mq = cur if mq is None else jnp.maximum(mq, cur)
                p = jnp.exp2(s)
                part = jnp.sum(p.reshape(t // sub, sub, t), axis=0)
                ls = part if ls is None else ls + part
                p_ref[sl * t:(sl + 1) * t, ql] = p.astype(BF16)
            ls_ref[:, ql] = ls
            mx = mq if mx is None else jnp.concatenate([mx, mq], axis=1)
        return mx

    def accumulate(c, p_ref, ls_ref):
        start = pl.multiple_of(c * tk, tk)
        for qh in range(tq // t):
            ql = slice(qh * t, (qh + 1) * t)
            pv = None
            for sl in range(tk // t):
                d = jnp.dot(vt_ref[:, pl.ds(start + sl * t, t)], p_ref[sl * t:(sl + 1) * t, ql],
                            preferred_element_type=F32)
                pv = d if pv is None else pv + d
            acc_ref[:, ql] += pv
        l_ref[...] += ls_ref[...]

    def checked_probabilities(c, p_ref, ls_ref):
        mx = probabilities(c, p_ref, ls_ref)

        @pl.when(jnp.max(mx) > ATTN_HEADROOM)
        def _():
            m_old = m_ref[...]
            m_new = bf16_round(m_old + jnp.maximum(mx, 0.0))
            shrink = jnp.exp2(m_old - m_new)
            acc_ref[...] *= shrink
            l_ref[...] *= shrink
            set_stabiliser(m_new)
            probabilities(c, p_ref, ls_ref)

    qa_ref[...] = qt_ref[...]
    s0 = jnp.dot(ka_ref[0:ATTN_ANCHOR_KEYS, :], qa_ref[...], preferred_element_type=F32)
    set_stabiliser(bf16_round(jnp.max(s0, axis=0, keepdims=True)))
    acc_ref[...] = jnp.zeros_like(acc_ref)
    l_ref[...] = jnp.zeros_like(l_ref)

    even, odd = (p0_ref, ls0_ref), (p1_ref, ls1_ref)
    checked_probabilities(0, *even)

    def body(j, carry):
        c = 2 * j
        accumulate(c, *even)
        checked_probabilities(c + 1, *odd)
        accumulate(c + 1, *odd)
        checked_probabilities(jnp.minimum(c + 2, n_chunks - 1), *even)
        return carry

    lax.fori_loop(0, n_chunks // 2, body, 0)
    denom = jnp.sum(l_ref[...], axis=0, keepdims=True)
    o_ref[...] = (acc_ref[...] / denom).T.astype(o_ref.dtype)


def _attention(qt, ka, vt, *, tq=2048, tk=2048):
    batch, heads, dqk, seq = qt.shape
    dv = vt.shape[2]
    assert seq % tq == 0 and seq % (2 * tk) == 0, "key chunks are consumed in pairs"
    return pl.pallas_call(
        functools.partial(_attn_kernel, tk),
        grid=(batch, heads, seq // tq),
        in_specs=[
            pl.BlockSpec((None, None, dqk, tq), lambda b, h, i: (b, h, 0, i)),
            pl.BlockSpec((None, None, seq, dqk), lambda b, h, i: (b, h, 0, 0)),
            pl.BlockSpec((None, None, dv, seq), lambda b, h, i: (b, h, 0, 0)),
        ],
        out_specs=pl.BlockSpec((None, tq, MLA_DV), lambda b, h, i: (b, i, h)),
        out_shape=jax.ShapeDtypeStruct((batch, seq, heads * MLA_DV), BF16),
        scratch_shapes=[pltpu.VMEM((dqk, tq), BF16), pltpu.VMEM((1, tq), F32),
                        pltpu.VMEM((dv, tq), F32), pltpu.VMEM((SUBLANES, tq), F32),
                        pltpu.VMEM((tk, tq), BF16), pltpu.VMEM((tk, tq), BF16),
                        pltpu.VMEM((SUBLANES, tq), F32), pltpu.VMEM((SUBLANES, tq), F32)],
        compiler_params=_cparams(("parallel", "parallel", "arbitrary")),
        name="mla_attn",
    )(qt, ka, vt)


def _outproj_kernel(h_ref, ys_ref, ym_ref, w_ref, o_ref):
    ws = ys_ref.shape[1]
    o_ref[...] = (h_ref[...]
                  + jnp.dot(ys_ref[...], w_ref[:ws, :], preferred_element_type=F32)
                  + jnp.dot(ym_ref[...], w_ref[ws:, :], preferred_element_type=F32))


def _outproj(h2d, ys, ym, w, tm=512):
    m, d = h2d.shape
    return pl.pallas_call(
        _outproj_kernel,
        grid=(m // tm,),
        in_specs=[
            pl.BlockSpec((tm, d), lambda i: (i, 0)),
            pl.BlockSpec((tm, ys.shape[1]), lambda i: (i, 0)),
            pl.BlockSpec((tm, ym.shape[1]), lambda i: (i, 0)),
            pl.BlockSpec(w.shape, lambda i: (0, 0)),
        ],
        out_specs=pl.BlockSpec((tm, d), lambda i: (i, 0)),
        out_shape=jax.ShapeDtypeStruct((m, d), F32),
        compiler_params=_cparams(("parallel",)),
        name="outproj",
    )(h2d, ys, ym, w)


def _ffn_kernel(final, h_ref, g_ref, gf_ref, w1_ref, w3_ref, w2_ref, o_ref, xn_ref, acc_ref):
    j = pl.program_id(1)

    @pl.when(j == 0)
    def _():
        xn_ref[...] = _rms(h_ref[...], g_ref[...]).astype(BF16)
        acc_ref[...] = jnp.zeros_like(acc_ref)

    xn = xn_ref[...]
    a = jnp.dot(xn, w1_ref[...], preferred_element_type=F32)
    b = jnp.dot(xn, w3_ref[...], preferred_element_type=F32)
    gated = (a * _sigmoid(a) * b).astype(BF16)
    acc_ref[...] += jnp.dot(gated, w2_ref[...], preferred_element_type=F32)

    @pl.when(j == pl.num_programs(1) - 1)
    def _():
        y = h_ref[...] + acc_ref[...]
        if final:
            y = _rms(y, gf_ref[...])
        o_ref[...] = y


def _ffn(h2d, gain, gain_final, w1, w3, w2, *, final, tm=512, tf=512):
    m, d = h2d.shape
    f = w1.shape[1]
    return pl.pallas_call(
        functools.partial(_ffn_kernel, final),
        grid=(m // tm, f // tf),
        in_specs=[
            pl.BlockSpec((tm, d), lambda i, j: (i, 0)),
            pl.BlockSpec((1, d), lambda i, j: (0, 0)),
            pl.BlockSpec((1, d), lambda i, j: (0, 0)),
            pl.BlockSpec((d, tf), lambda i, j: (0, j)),
            pl.BlockSpec((d, tf), lambda i, j: (0, j)),
            pl.BlockSpec((tf, d), lambda i, j: (j, 0)),
        ],
        out_specs=pl.BlockSpec((tm, d), lambda i, j: (i, 0)),
        out_shape=jax.ShapeDtypeStruct((m, d), F32),
        scratch_shapes=[pltpu.VMEM((tm, d), BF16), pltpu.VMEM((tm, d), F32)],
        compiler_params=_cparams(("parallel", "arbitrary")),
        name="ffn",
    )(h2d, gain, gain_final, w1, w3, w2)


def _pad_heads(w, heads, dk):
    lead = w.shape[:-1]
    w = w.reshape(lead + (heads, dk))
    w = jnp.pad(w, [(0, 0)] * len(lead) + [(0, 0), (0, LANES - dk)])
    return w.reshape(lead + (heads * LANES,))


def _layout_w_in(w):
    w = w.astype(BF16)
    sizes = (256, 256, 512, 16, 16, 512, 512, 512, 512, 512, 512, 512, 512, 64)
    (gq, gk, gv, gaf, gab, gg, hq, hff, hfb, hi, hg, cq, ckv, kr) = jnp.split(
        w, np.cumsum(sizes)[:-1].tolist(), axis=1)
    half = MLA_ROPE // 2
    kr_swapped = jnp.concatenate([kr[:, half:], kr[:, :half]], axis=1)
    d = w.shape[0]
    cols = [_pad_heads(gq, GLA_HEADS, GLA_DK), _pad_heads(gk, GLA_HEADS, GLA_DK), gv, gg,
            hq, hff, hfb, hi, hg, cq, ckv, kr, kr_swapped,
            gaf, gab, jnp.zeros((d, LANES - 2 * GLA_GATE_RANK), w.dtype)]
    out = jnp.concatenate(cols, axis=1)
    return jnp.pad(out, [(0, 0), (0, U_WIDTH - out.shape[1])])


def _layout_w_qb(w):
    r = w.shape[0]
    w = w.astype(BF16).reshape(r, MLA_HEADS, MLA_DQK)
    rope = w[:, :, MLA_NOPE:]
    half = MLA_ROPE // 2
    swapped = jnp.concatenate([rope[:, :, half:], rope[:, :, :half]], axis=2)
    return jnp.concatenate([w, swapped], axis=2).reshape(r, MLA_HEADS * 256)


def _tri(tb, reverse):
    r = np.arange(tb)[:, None]
    c = np.arange(tb)[None, :]
    same = (r // SCAN_CHUNK) == (c // SCAN_CHUNK)
    keep = (c >= r) if reverse else (c <= r)
    return jnp.asarray((same & keep).astype(np.float32)).astype(BF16)


def kernel(x, positions, w_in, gla_gate_up, gla_gate_bias, gla_out_norm, hgrn_lb_logits,
           hgrn_out_norm, mla_qa_norm, mla_w_qb, mla_kva_norm, mla_w_kvb, w_out,
           norm_mix, norm_ffn, ffn_w1, ffn_w3, ffn_w2, norm_final):
    batch, seq, d = x.shape
    depth = w_in.shape[0]
    m = batch * seq

    inv_freq = 1.0 / (ROPE_THETA ** (jnp.arange(0, MLA_ROPE, 2, dtype=F32) / MLA_ROPE))
    ang = positions.astype(F32)[..., None] * inv_freq
    cos, sin = jnp.cos(ang), jnp.sin(ang)
    cs = jnp.concatenate([cos, cos, -sin, sin], axis=-1).reshape(m, LANES)

    tri_f, tri_b = _tri(SCAN_BLOCK, False), _tri(SCAN_BLOCK, True)
    h = x.reshape(m, d)
    for l in range(depth):
        u = _inproj(h, norm_mix[l].reshape(1, d), _layout_w_in(w_in[l]))

        def gate_w(direction):
            wgd = _pad_heads(gla_gate_up[l, direction], GLA_HEADS, GLA_DK)
            lo = direction * GLA_GATE_RANK
            return jnp.pad(wgd, [(lo, LANES - lo - GLA_GATE_RANK), (0, 0)])

        gb = _pad_heads(gla_gate_bias[l], GLA_HEADS, GLA_DK)
        onorm = jnp.concatenate([jnp.tile(gla_out_norm[l], GLA_HEADS),
                                 jnp.tile(hgrn_out_norm[l], HGRN_HEADS)]).reshape(1, SCAN_WIDTH)
        o_b = _scan(u, None, gate_w(1), gb[1:2], hgrn_lb_logits[1], tri_b, None,
                    batch=batch, seq=seq, layer=l, reverse=True)
        y_scan = _scan(u, o_b, gate_w(0), gb[0:1], hgrn_lb_logits[0], tri_f, onorm,
                       batch=batch, seq=seq, layer=l, reverse=False)

        q, k, vt = _mla_prep(u, cs, mla_qa_norm[l].reshape(1, -1), mla_kva_norm[l].reshape(1, -1),
                             _layout_w_qb(mla_w_qb[l]), mla_w_kvb[l].astype(BF16),
                             batch=batch, seq=seq)
        y_mla = _attention(q, k, vt).reshape(m, MLA_HEADS * MLA_DV)

        h = _outproj(h, y_scan, y_mla, w_out[l].astype(BF16))
        h = _ffn(h, norm_ffn[l].reshape(1, d), norm_final.reshape(1, d),
                 ffn_w1[l].astype(BF16), ffn_w3[l].astype(BF16), ffn_w2[l].astype(BF16),
                 final=(l == depth - 1))
    return h.reshape(batch, seq, d)
```

```python
import functools

import jax
import jax.numpy as jnp
import numpy as np
from jax import lax
from jax.experimental import pallas as pl
from jax.experimental.pallas import tpu as pltpu

D_MODEL = 2048
GLA_HEADS, GLA_DK, GLA_DV = 4, 64, 128
GLA_GATE_RANK = 16
GLA_GATE_TEMP = 16.0
HGRN_HEADS, HGRN_DK, HGRN_DV = 4, 128, 128
MLA_HEADS, MLA_Q_RANK, MLA_KV_RANK = 8, 512, 512
MLA_NOPE, MLA_ROPE, MLA_DV = 128, 64, 128
MLA_DQK = MLA_NOPE + MLA_ROPE
ROPE_THETA = 10000.0
EPS = 1e-6

LANES = 128
SUBLANES = 8
MXU_TILE = 256
SCAN_HEADS = GLA_HEADS + HGRN_HEADS
SCAN_WIDTH = SCAN_HEADS * LANES
GLA_W = GLA_HEADS * LANES
HGRN_W = HGRN_HEADS * LANES

CB_GQ, CB_GK, CB_GV, CB_GG, CB_HQ, CB_HFF, CB_HFB, CB_HI, CB_HG, CB_CQ, CB_CKV = range(11)
CB128_KR = 44
CB128_GA = 45
U_WIDTH = 6144

SCAN_CHUNK = 32
SCAN_BLOCK = 256
SCAN_FACTOR_LIMIT = 60.0
VMEM_LIMIT = 56 * 1024 * 1024

F32 = jnp.float32
BF16 = jnp.bfloat16


def _cparams(sem):
    return pltpu.CompilerParams(dimension_semantics=sem, vmem_limit_bytes=VMEM_LIMIT)


def _rms(x, gain):
    return x * lax.rsqrt(jnp.mean(x * x, axis=-1, keepdims=True) + EPS) * gain


def _sigmoid(z):
    return 1.0 / (1.0 + jnp.exp(-z))


def _softplus_neg_abs(z):
    t = jnp.exp(-jnp.abs(z))
    return t, jnp.log(1.0 + t)


def _split_bf16(x, n):
    parts = []
    for _ in range(n):
        p = x.astype(BF16)
        parts.append(p)
        x = x - p.astype(F32)
    return parts


def _inproj_kernel(x_ref, g_ref, w_ref, o_ref, xn_ref):
    @pl.when(pl.program_id(1) == 0)
    def _():
        xn_ref[...] = _rms(x_ref[...], g_ref[...]).astype(BF16)

    tn = o_ref.shape[1]
    cols = pl.ds(pl.multiple_of(pl.program_id(1) * tn, tn), tn)
    o_ref[...] = jnp.dot(xn_ref[...], w_ref[:, cols], preferred_element_type=F32)


def _inproj(h2d, gain, w, tm=512, tn=1536):
    m, d = h2d.shape
    n = w.shape[1]
    return pl.pallas_call(
        _inproj_kernel,
        grid=(m // tm, n // tn),
        in_specs=[
            pl.BlockSpec((tm, d), lambda i, j: (i, 0)),
            pl.BlockSpec((1, d), lambda i, j: (0, 0)),
            pl.BlockSpec((d, n), lambda i, j: (0, 0), pipeline_mode=pl.Buffered(1)),
        ],
        out_specs=pl.BlockSpec((tm, tn), lambda i, j: (i, j)),
        out_shape=jax.ShapeDtypeStruct((m, n), F32),
        scratch_shapes=[pltpu.VMEM((tm, d), BF16)],
        compiler_params=_cparams(("parallel", "arbitrary")),
        name="inproj",
    )(h2d, gain, w)


def _scan_kernel(reverse, layer, *refs):
    if reverse:
        (gq, gk, gv, ga, hq, hf, hi, wg, gb, lbl, tri,
         out, qs, ks, gs, st) = refs
    else:
        (gq, gk, gv, gg, ga, hq, hf, hi, hg, ob, wg, gb, lbl, tri, onorm,
         out, qs, ks, gs, st, osum) = refs
    tb = qs.shape[0]
    n_chunks = tb // SCAN_CHUNK
    c = SCAN_CHUNK

    @pl.when(pl.program_id(1) == 0)
    def _():
        st[...] = jnp.zeros_like(st)

    qs[:, :GLA_W] = gq[...] * (GLA_DK ** -0.5)
    hqv = hq[...]
    qs[:, GLA_W:] = hqv * _sigmoid(hqv)
    ga_hi, ga_lo = _split_bf16(ga[...], 2)
    wg_hi, wg_lo = _split_bf16(wg[...], 2)
    z = (jnp.dot(ga_hi, wg_hi, preferred_element_type=F32)
         + jnp.dot(ga_lo, wg_hi, preferred_element_type=F32)
         + jnp.dot(ga_hi, wg_lo, preferred_element_type=F32)) + gb[...]
    lg_gla = (jnp.minimum(z, 0.0) - _softplus_neg_abs(z)[1]) * (1.0 / GLA_GATE_TEMP)
    logits = lbl[...]
    e = jnp.exp(logits - jnp.max(logits, axis=0, keepdims=True))
    sm = e / jnp.sum(e, axis=0, keepdims=True)
    lb = jnp.zeros_like(sm[0:1])
    for dd in range(1, layer + 1):
        lb = lb + sm[dd:dd + 1]
    zf = hf[...]
    t, sp = _softplus_neg_abs(zf)
    ks[:, :GLA_W] = gk[...]
    ks[:, GLA_W:] = (1.0 - lb) * (jnp.where(zf > 0.0, t, 1.0) / (1.0 + t))
    a = jnp.log(lb)
    b = jnp.log1p(-lb) + (jnp.minimum(zf, 0.0) - sp)
    lg_h = jnp.maximum(a, b) + _softplus_neg_abs(a - b)[1]
    tri_b = tri[...]
    gs[:, :GLA_W] = sum(jnp.dot(tri_b, part, preferred_element_type=F32)
                        for part in _split_bf16(lg_gla, 3))
    gs[:, GLA_W:] = sum(jnp.dot(tri_b, part, preferred_element_type=F32)
                        for part in _split_bf16(lg_h, 3))

    row = lax.broadcasted_iota(jnp.int32, (c, LANES), 0)
    ones = jnp.ones((LANES, LANES), BF16)
    ii = lax.broadcasted_iota(jnp.int32, (c, c), 0)
    jj = lax.broadcasted_iota(jnp.int32, (c, c), 1)
    live_cc = (ii <= jj) if reverse else (ii >= jj)
    o_dst = out if reverse else osum
    nt_dims = (((1,), (1,)), ((), ()))

    def intra_direct(q, k, g, v):
        parts = []
        for j in range(c):
            live = (row <= j) if reverse else (row >= j)
            dec = jnp.exp(jnp.where(live, g - g[j:j + 1], -jnp.inf))
            parts.append((q * dec * k[j:j + 1]).astype(BF16))
        att = jnp.dot(jnp.concatenate(parts, axis=0), ones,
                      preferred_element_type=F32)
        o_intra = att[0:c] * v[0:1]
        for j in range(1, c):
            o_intra = o_intra + att[j * c:(j + 1) * c] * v[j:j + 1]
        return o_intra

    tn_dims = (((0,), (0,)), ((), ()))

    def values(h, rs):
        if h < GLA_HEADS:
            return gv[rs, h * LANES:(h + 1) * LANES]
        return hi[rs, (h - GLA_HEADS) * LANES:(h - GLA_HEADS + 1) * LANES]

    def block_factored():
        order = range(n_chunks - 1, -1, -1) if reverse else range(n_chunks)
        for h in range(SCAN_HEADS):
            hl = slice(h * LANES, (h + 1) * LANES)
            staged = []
            for cidx in order:
                rs = slice(cidx * c, (cidx + 1) * c)
                q, k, g = qs[rs, hl], ks[rs, hl], gs[rs, hl]
                v = values(h, rs).astype(BF16)
                g_near = g[c - 1:c] if reverse else g[0:1]
                g_far = g[0:1] if reverse else g[c - 1:c]
                att = lax.dot_general((q * jnp.exp(g - g_near)).astype(BF16),
                                      (k * jnp.exp(g_near - g)).astype(BF16),
                                      nt_dims, preferred_element_type=F32)
                kv = lax.dot_general(v, (k * jnp.exp(g_far - g)).astype(BF16),
                                     tn_dims, preferred_element_type=F32)
                staged.append((rs, (q * jnp.exp(g)).astype(BF16), att, v, jnp.exp(g_far), kv))
            s_t = st[h]
            for rs, q_dec, att, v, decay, kv in staged:
                o_inter = lax.dot_general(q_dec, s_t.astype(BF16), nt_dims,
                                          preferred_element_type=F32)
                o_intra = jnp.dot(jnp.where(live_cc, att, 0.0).astype(BF16), v,
                                  preferred_element_type=F32)
                o_dst[rs, hl] = o_inter + o_intra
                s_t = s_t * decay + kv
            st[h] = s_t

    def chunk_direct(ci, carry):
        cidx = (n_chunks - 1 - ci) if reverse else ci
        rs = pl.ds(pl.multiple_of(cidx * c, c), c)
        for h in range(SCAN_HEADS):
            hl = slice(h * LANES, (h + 1) * LANES)
            q, k, g = qs[rs, hl], ks[rs, hl], gs[rs, hl]
            v = values(h, rs)
            g_far = g[0:1] if reverse else g[c - 1:c]
            s_t = st[h]
            o_inter = lax.dot_general((q * jnp.exp(g)).astype(BF16), s_t.astype(BF16),
                                      nt_dims, preferred_element_type=F32)
            o_dst[rs, hl] = o_inter + intra_direct(q, k, g, v)
            st[h] = s_t * jnp.exp(g_far) + lax.dot_general(
                v.astype(BF16), (k * jnp.exp(g_far - g)).astype(BF16), tn_dims,
                preferred_element_type=F32)
        return carry

    bounded = jnp.min(gs[...]) >= -SCAN_FACTOR_LIMIT
    pl.when(bounded)(block_factored)

    @pl.when(jnp.logical_not(bounded))
    def _():
        lax.fori_loop(0, n_chunks, chunk_direct, 0)

    if not reverse:
        for h in range(SCAN_HEADS):
            hl = slice(h * LANES, (h + 1) * LANES)
            o = osum[:, hl] + ob[:, hl]
            if h < GLA_HEADS:
                gate = gg[:, hl]
            else:
                gate = hg[:, (h - GLA_HEADS) * LANES:(h - GLA_HEADS + 1) * LANES]
            y = _rms(o, onorm[:, hl]) * (gate * _sigmoid(gate))
            out[:, hl] = y.astype(out.dtype)


def _scan(u, o_b, wg, gb, lbl, tri, onorm, *, batch, seq, layer, reverse):
    tb = SCAN_BLOCK
    nt = seq // tb

    def rows(b, t):
        return b * nt + ((nt - 1 - t) if reverse else t)

    def ublock(cb):
        return pl.BlockSpec((tb, 512), lambda b, t: (rows(b, t), cb))

    def const(shape):
        return pl.BlockSpec(shape, lambda b, t: (0,) * len(shape))

    ga_spec = pl.BlockSpec((tb, LANES), lambda b, t: (rows(b, t), CB128_GA))
    wide = pl.BlockSpec((tb, SCAN_WIDTH), lambda b, t: (rows(b, t), 0))
    params = [const((LANES, GLA_W)), const((1, GLA_W)), const(lbl.shape), const((tb, tb))]
    scratch = [pltpu.VMEM((tb, SCAN_WIDTH), F32)] * 3 + [pltpu.VMEM((SCAN_HEADS, LANES, LANES), F32)]
    if reverse:
        in_specs = [ublock(CB_GQ), ublock(CB_GK), ublock(CB_GV), ga_spec,
                    ublock(CB_HQ), ublock(CB_HFB), ublock(CB_HI)] + params
        args = [u, u, u, u, u, u, u, wg, gb, lbl, tri]
        out_dtype = F32
    else:
        in_specs = [ublock(CB_GQ), ublock(CB_GK), ublock(CB_GV), ublock(CB_GG), ga_spec,
                    ublock(CB_HQ), ublock(CB_HFF), ublock(CB_HI), ublock(CB_HG), wide] + params + [
                        const((1, SCAN_WIDTH))]
        args = [u, u, u, u, u, u, u, u, u, o_b, wg, gb, lbl, tri, onorm]
        out_dtype = BF16
        scratch = scratch + [pltpu.VMEM((tb, SCAN_WIDTH), F32)]
    return pl.pallas_call(
        functools.partial(_scan_kernel, reverse, layer),
        grid=(batch, nt),
        in_specs=in_specs,
        out_specs=wide,
        out_shape=jax.ShapeDtypeStruct((batch * seq, SCAN_WIDTH), out_dtype),
        scratch_shapes=scratch,
        compiler_params=_cparams(("arbitrary", "arbitrary")),
        name="scan_bwd" if reverse else "scan_fwd",
    )(*args)


ATTN_DQK_PAD = 256
ATTN_STAB_ROW = MLA_DQK


def _mla_prep_kernel(cq_ref, ckv_ref, kr_ref, cs_ref, gq_ref, gkv_ref, wq_ref, wkv_ref,
                     qt_out, ka_out, vt_out):
    scale = MLA_DQK ** -0.5 * float(np.log2(np.e))
    tm = cq_ref.shape[0]
    cs = cs_ref[...]
    yq = jnp.dot(_rms(cq_ref[...], gq_ref[...]).astype(BF16), wq_ref[...],
                 preferred_element_type=F32)
    ykv = jnp.dot(_rms(ckv_ref[...], gkv_ref[...]).astype(BF16), wkv_ref[...],
                  preferred_element_type=F32)
    lane = lax.broadcasted_iota(jnp.int32, (tm, LANES), 1)
    is_rope = lane < MLA_ROPE

    def rotary(t):
        t = t * cs
        return t + pltpu.roll(t, MLA_ROPE, 1)

    k_hi = jnp.where(is_rope, rotary(kr_ref[...]),
                     (lane == MLA_ROPE).astype(F32)).astype(BF16)
    for h in range(MLA_HEADS):
        yh = yq[:, h * 256:(h + 1) * 256]
        q_hi = jnp.where(is_rope, rotary(yh[:, MLA_NOPE:]), 0.0)
        q_full = jnp.concatenate([yh[:, :MLA_NOPE], q_hi], axis=1) * scale
        qt_out[h] = q_full.T.astype(BF16)
        kvh = ykv[:, h * 256:(h + 1) * 256]
        ka_out[h, :, :MLA_NOPE] = kvh[:, :MLA_NOPE].astype(BF16)
        ka_out[h, :, MLA_NOPE:] = k_hi
        vt_out[h] = kvh[:, MLA_NOPE:].T.astype(BF16)


def _mla_prep(u, cs, gq, gkv, wq, wkv, *, batch, seq, tm=512):
    nt = seq // tm

    def rows(b, t):
        return b * nt + t

    def const(shape):
        return pl.BlockSpec(shape, lambda b, t: (0,) * len(shape))

    return pl.pallas_call(
        _mla_prep_kernel,
        grid=(batch, nt),
        in_specs=[
            pl.BlockSpec((tm, 512), lambda b, t: (rows(b, t), CB_CQ)),
            pl.BlockSpec((tm, 512), lambda b, t: (rows(b, t), CB_CKV)),
            pl.BlockSpec((tm, LANES), lambda b, t: (rows(b, t), CB128_KR)),
            pl.BlockSpec((tm, LANES), lambda b, t: (rows(b, t), 0)),
            const((1, MLA_Q_RANK)), const((1, MLA_KV_RANK)),
            const(wq.shape), const(wkv.shape),
        ],
        out_specs=[
            pl.BlockSpec((None, MLA_HEADS, ATTN_DQK_PAD, tm), lambda b, t: (b, 0, 0, t)),
            pl.BlockSpec((None, MLA_HEADS, tm, ATTN_DQK_PAD), lambda b, t: (b, 0, t, 0)),
            pl.BlockSpec((None, MLA_HEADS, MLA_DV, tm), lambda b, t: (b, 0, 0, t)),
        ],
        out_shape=[
            jax.ShapeDtypeStruct((batch, MLA_HEADS, ATTN_DQK_PAD, seq), BF16),
            jax.ShapeDtypeStruct((batch, MLA_HEADS, seq, ATTN_DQK_PAD), BF16),
            jax.ShapeDtypeStruct((batch, MLA_HEADS, MLA_DV, seq), BF16),
        ],
        compiler_params=_cparams(("parallel", "parallel")),
        name="mla_prep",
    )(u, u, u, cs, gq, gkv, wq, wkv)


ATTN_HEADROOM = 32.0


def _attn_kernel(tk, qt_ref, ka_ref, vt_ref, o_ref, qa_ref, m_ref, acc_ref, l_ref,
                 p0_ref, p1_ref, ls0_ref, ls1_ref):
    tq = qt_ref.shape[1]
    n_chunks = ka_ref.shape[0] // tk
    t = MXU_TILE
    sub = l_ref.shape[0]
    stab_rows = slice(ATTN_STAB_ROW, ATTN_STAB_ROW + 16)
    row0 = lax.broadcasted_iota(jnp.int32, (16, tq), 0) == 0

    def set_stabiliser(m):
        m_ref[...] = m
        qa_ref[stab_rows, :] = jnp.where(row0, -m, 0.0).astype(BF16)

    def bf16_round(x):
        return x.astype(BF16).astype(F32)

    def probabilities(c, p_ref, ls_ref):
        start = pl.multiple_of(c * tk, tk)
        mx = None
        for qh in range(tq // t):
            ql = slice(qh * t, (qh + 1) * t)
            mq, ls = None, None
            for sl in range(tk // t):
                s = jnp.dot(ka_ref[pl.ds(start + sl * t, t), :], qa_ref[:, ql],
                            preferred_element_type=F32)
                cur = jnp.max(s, axis=0, keepdims=True)
                mq = cur if mq is None else jnp.maximum(mq, cur)
                p = jnp.exp2(s)
                part = jnp.sum(p.reshape(t // sub, sub, t), axis=0)
                ls = part if ls is None else ls + part
                p_ref[sl * t:(sl + 1) * t, ql] = p.astype(BF16)
            ls_ref[:, ql] = ls
            mx = mq if mx is None else jnp.concatenate([mx, mq], axis=1)
        return mx

    def accumulate(c, p_ref, ls_ref):
        start = pl.multiple_of(c * tk, tk)
        for qh in range(tq // t):
            ql = slice(qh * t, (qh + 1) * t)
            pv = None
            for sl in range(tk // t):
                d = jnp.dot(vt_ref[:, pl.ds(start + sl * t, t)], p_ref[sl * t:(sl + 1) * t, ql],
                            preferred_element_type=F32)
                pv = d if pv is None else pv + d
            acc_ref[:, ql] += pv
        l_ref[...] += ls_ref[...]

    def checked_probabilities(c, p_ref, ls_ref, first=False):
        mx = probabilities(c, p_ref, ls_ref)
        bad = jnp.max(mx) > ATTN_HEADROOM
        if first:
            bad = jnp.logical_or(bad, jnp.min(mx) < -ATTN_HEADROOM)

        @pl.when(bad)
        def _():
            m_old = m_ref[...]
            m_new = bf16_round(m_old + (mx if first else jnp.maximum(mx, 0.0)))
            if not first:
                shrink = jnp.exp2(m_old - m_new)
                acc_ref[...] *= shrink
                l_ref[...] *= shrink
            set_stabiliser(m_new)
            probabilities(c, p_ref, ls_ref)

    qa_ref[...] = qt_ref[...]
    m_ref[...] = jnp.zeros_like(m_ref)
    acc_ref[...] = jnp.zeros_like(acc_ref)
    l_ref[...] = jnp.zeros_like(l_ref)

    even, odd = (p0_ref, ls0_ref), (p1_ref, ls1_ref)
    checked_probabilities(0, *even, first=True)

    def pair(c, last):
        accumulate(c, *even)
        checked_probabilities(c + 1, *odd)
        accumulate(c + 1, *odd)
        if not last:
            checked_probabilities(c + 2, *even)

    def body(j, carry):
        pair(2 * j, last=False)
        return carry

    lax.fori_loop(0, n_chunks // 2 - 1, body, 0)
    pair(n_chunks - 2, last=True)
    denom = jnp.sum(l_ref[...], axis=0, keepdims=True)
    o_ref[...] = (acc_ref[...] / denom).T.astype(o_ref.dtype)


def _attention(qt, ka, vt, *, tq=2048, tk=2048):
    batch, heads, dqk, seq = qt.shape
    dv = vt.shape[2]
    assert seq % tq == 0 and seq % (2 * tk) == 0, "key chunks are consumed in pairs"
    return pl.pallas_call(
        functools.partial(_attn_kernel, tk),
        grid=(batch, heads, seq // tq),
        in_specs=[
            pl.BlockSpec((None, None, dqk, tq), lambda b, h, i: (b, h, 0, i)),
            pl.BlockSpec((None, None, seq, dqk), lambda b, h, i: (b, h, 0, 0)),
            pl.BlockSpec((None, None, dv, seq), lambda b, h, i: (b, h, 0, 0)),
        ],
        out_specs=pl.BlockSpec((None, tq, MLA_DV), lambda b, h, i: (b, i, h)),
        out_shape=jax.ShapeDtypeStruct((batch, seq, heads * MLA_DV), BF16),
        scratch_shapes=[pltpu.VMEM((dqk, tq), BF16), pltpu.VMEM((1, tq), F32),
                        pltpu.VMEM((dv, tq), F32), pltpu.VMEM((SUBLANES, tq), F32),
                        pltpu.VMEM((tk, tq), BF16), pltpu.VMEM((tk, tq), BF16),
                        pltpu.VMEM((SUBLANES, tq), F32), pltpu.VMEM((SUBLANES, tq), F32)],
        compiler_params=_cparams(("parallel", "parallel", "arbitrary")),
        name="mla_attn",
    )(qt, ka, vt)


def _outproj_kernel(h_ref, ys_ref, ym_ref, w_ref, o_ref):
    ws = ys_ref.shape[1]
    o_ref[...] = (h_ref[...]
                  + jnp.dot(ys_ref[...], w_ref[:ws, :], preferred_element_type=F32)
                  + jnp.dot(ym_ref[...], w_ref[ws:, :], preferred_element_type=F32))


def _outproj(h2d, ys, ym, w, tm=512):
    m, d = h2d.shape
    return pl.pallas_call(
        _outproj_kernel,
        grid=(m // tm,),
        in_specs=[
            pl.BlockSpec((tm, d), lambda i: (i, 0)),
            pl.BlockSpec((tm, ys.shape[1]), lambda i: (i, 0)),
            pl.BlockSpec((tm, ym.shape[1]), lambda i: (i, 0)),
            pl.BlockSpec(w.shape, lambda i: (0, 0)),
        ],
        out_specs=pl.BlockSpec((tm, d), lambda i: (i, 0)),
        out_shape=jax.ShapeDtypeStruct((m, d), F32),
        compiler_params=_cparams(("parallel",)),
        name="outproj",
    )(h2d, ys, ym, w)


def _ffn_kernel(final, h_ref, g_ref, gf_ref, w1_ref, w3_ref, w2_ref, o_ref, xn_ref, acc_ref):
    j = pl.program_id(1)

    @pl.when(j == 0)
    def _():
        xn_ref[...] = _rms(h_ref[...], g_ref[...]).astype(BF16)
        acc_ref[...] = jnp.zeros_like(acc_ref)

    xn = xn_ref[...]
    a = jnp.dot(xn, w1_ref[...], preferred_element_type=F32)
    b = jnp.dot(xn, w3_ref[...], preferred_element_type=F32)
    gated = (a * _sigmoid(a) * b).astype(BF16)
    acc_ref[...] += jnp.dot(gated, w2_ref[...], preferred_element_type=F32)

    @pl.when(j == pl.num_programs(1) - 1)
    def _():
        y = h_ref[...] + acc_ref[...]
        if final:
            y = _rms(y, gf_ref[...])
        o_ref[...] = y


def _ffn(h2d, gain, gain_final, w1, w3, w2, *, final, tm=512, tf=512):
    m, d = h2d.shape
    f = w1.shape[1]
    return pl.pallas_call(
        functools.partial(_ffn_kernel, final),
        grid=(m // tm, f // tf),
        in_specs=[
            pl.BlockSpec((tm, d), lambda i, j: (i, 0)),
            pl.BlockSpec((1, d), lambda i, j: (0, 0)),
            pl.BlockSpec((1, d), lambda i, j: (0, 0)),
            pl.BlockSpec((d, tf), lambda i, j: (0, j)),
            pl.BlockSpec((d, tf), lambda i, j: (0, j)),
            pl.BlockSpec((tf, d), lambda i, j: (j, 0)),
        ],
        out_specs=pl.BlockSpec((tm, d), lambda i, j: (i, 0)),
        out_shape=jax.ShapeDtypeStruct((m, d), F32),
        scratch_shapes=[pltpu.VMEM((tm, d), BF16), pltpu.VMEM((tm, d), F32)],
        compiler_params=_cparams(("parallel", "arbitrary")),
        name="ffn",
    )(h2d, gain, gain_final, w1, w3, w2)


def _pad_heads(w, heads, dk):
    lead = w.shape[:-1]
    w = w.reshape(lead + (heads, dk))
    w = jnp.pad(w, [(0, 0)] * len(lead) + [(0, 0), (0, LANES - dk)])
    return w.reshape(lead + (heads * LANES,))


def _layout_w_in(w):
    w = w.astype(BF16)
    sizes = (256, 256, 512, 16, 16, 512, 512, 512, 512, 512, 512, 512, 512, 64)
    (gq, gk, gv, gaf, gab, gg, hq, hff, hfb, hi, hg, cq, ckv, kr) = jnp.split(
        w, np.cumsum(sizes)[:-1].tolist(), axis=1)
    half = MLA_ROPE // 2
    kr_swapped = jnp.concatenate([kr[:, half:], kr[:, :half]], axis=1)
    d = w.shape[0]
    cols = [_pad_heads(gq, GLA_HEADS, GLA_DK), _pad_heads(gk, GLA_HEADS, GLA_DK), gv, gg,
            hq, hff, hfb, hi, hg, cq, ckv, kr, kr_swapped,
            gaf, gab, jnp.zeros((d, LANES - 2 * GLA_GATE_RANK), w.dtype)]
    out = jnp.concatenate(cols, axis=1)
    return jnp.pad(out, [(0, 0), (0, U_WIDTH - out.shape[1])])


def _layout_w_qb(w):
    r = w.shape[0]
    w = w.astype(BF16).reshape(r, MLA_HEADS, MLA_DQK)
    rope = w[:, :, MLA_NOPE:]
    half = MLA_ROPE // 2
    swapped = jnp.concatenate([rope[:, :, half:], rope[:, :, :half]], axis=2)
    return jnp.concatenate([w, swapped], axis=2).reshape(r, MLA_HEADS * 256)


def _tri(tb, reverse):
    r = np.arange(tb)[:, None]
    c = np.arange(tb)[None, :]
    same = (r // SCAN_CHUNK) == (c // SCAN_CHUNK)
    keep = (c >= r) if reverse else (c <= r)
    return jnp.asarray((same & keep).astype(np.float32)).astype(BF16)


def kernel(x, positions, w_in, gla_gate_up, gla_gate_bias, gla_out_norm, hgrn_lb_logits,
           hgrn_out_norm, mla_qa_norm, mla_w_qb, mla_kva_norm, mla_w_kvb, w_out,
           norm_mix, norm_ffn, ffn_w1, ffn_w3, ffn_w2, norm_final):
    batch, seq, d = x.shape
    depth = w_in.shape[0]
    m = batch * seq

    inv_freq = 1.0 / (ROPE_THETA ** (jnp.arange(0, MLA_ROPE, 2, dtype=F32) / MLA_ROPE))
    ang = positions.astype(F32)[..., None] * inv_freq
    cos, sin = jnp.cos(ang), jnp.sin(ang)
    cs = jnp.concatenate([cos, cos, -sin, sin], axis=-1).reshape(m, LANES)

    tri_f, tri_b = _tri(SCAN_BLOCK, False), _tri(SCAN_BLOCK, True)
    h = x.reshape(m, d)
    for l in range(depth):
        u = _inproj(h, norm_mix[l].reshape(1, d), _layout_w_in(w_in[l]))

        def gate_w(direction):
            wgd = _pad_heads(gla_gate_up[l, direction], GLA_HEADS, GLA_DK)
            lo = direction * GLA_GATE_RANK
            return jnp.pad(wgd, [(lo, LANES - lo - GLA_GATE_RANK), (0, 0)])

        gb = _pad_heads(gla_gate_bias[l], GLA_HEADS, GLA_DK)
        onorm = jnp.concatenate([jnp.tile(gla_out_norm[l], GLA_HEADS),
                                 jnp.tile(hgrn_out_norm[l], HGRN_HEADS)]).reshape(1, SCAN_WIDTH)
        o_b = _scan(u, None, gate_w(1), gb[1:2], hgrn_lb_logits[1], tri_b, None,
                    batch=batch, seq=seq, layer=l, reverse=True)
        y_scan = _scan(u, o_b, gate_w(0), gb[0:1], hgrn_lb_logits[0], tri_f, onorm,
                       batch=batch, seq=seq, layer=l, reverse=False)

        q, k, vt = _mla_prep(u, cs, mla_qa_norm[l].reshape(1, -1), mla_kva_norm[l].reshape(1, -1),
                             _layout_w_qb(mla_w_qb[l]), mla_w_kvb[l].astype(BF16),
                             batch=batch, seq=seq)
        y_mla = _attention(q, k, vt).reshape(m, MLA_HEADS * MLA_DV)

        h = _outproj(h, y_scan, y_mla, w_out[l].astype(BF16))
        h = _ffn(h, norm_ffn[l].reshape(1, d), norm_final.reshape(1, d),
                 ffn_w1[l].astype(BF16), ffn_w3[l].astype(BF16), ffn_w2[l].astype(BF16),
                 final=(l == depth - 1))
    return h.reshape(batch, seq, d)
```

```python
import functools

import jax
import jax.numpy as jnp
import numpy as np
from jax import lax
from jax.experimental import pallas as pl
from jax.experimental.pallas import tpu as pltpu

D_MODEL = 2048
GLA_HEADS, GLA_DK, GLA_DV = 4, 64, 128
GLA_GATE_RANK = 16
GLA_GATE_TEMP = 16.0
HGRN_HEADS, HGRN_DK, HGRN_DV = 4, 128, 128
MLA_HEADS, MLA_Q_RANK, MLA_KV_RANK = 8, 512, 512
MLA_NOPE, MLA_ROPE, MLA_DV = 128, 64, 128
MLA_DQK = MLA_NOPE + MLA_ROPE
ROPE_THETA = 10000.0
EPS = 1e-6

LANES = 128
SUBLANES = 8
MXU_TILE = 256
SCAN_HEADS = GLA_HEADS + HGRN_HEADS
SCAN_WIDTH = SCAN_HEADS * LANES
GLA_W = GLA_HEADS * LANES
HGRN_W = HGRN_HEADS * LANES

CB_GQ, CB_GK, CB_GV, CB_GG, CB_HQ, CB_HFF, CB_HFB, CB_HI, CB_HG, CB_CQ, CB_CKV = range(11)
CB128_KR = 44
CB128_GA = 45
U_WIDTH = 6144

SCAN_CHUNK = 32
SCAN_BLOCK = 512
SCAN_FACTOR_LIMIT = 60.0
VMEM_LIMIT = 56 * 1024 * 1024

F32 = jnp.float32
BF16 = jnp.bfloat16


def _cparams(sem):
    return pltpu.CompilerParams(dimension_semantics=sem, vmem_limit_bytes=VMEM_LIMIT)


def _rms(x, gain):
    return x * lax.rsqrt(jnp.mean(x * x, axis=-1, keepdims=True) + EPS) * gain


def _sigmoid(z):
    return 1.0 / (1.0 + jnp.exp(-z))


def _softplus_neg_abs(z):
    t = jnp.exp(-jnp.abs(z))
    return t, jnp.log(1.0 + t)


def _split_bf16(x, n):
    parts = []
    for _ in range(n):
        p = x.astype(BF16)
        parts.append(p)
        x = x - p.astype(F32)
    return parts


def _inproj_kernel(x_ref, g_ref, w_ref, o_ref, xn_ref):
    @pl.when(pl.program_id(1) == 0)
    def _():
        xn_ref[...] = _rms(x_ref[...], g_ref[...]).astype(BF16)

    tn = o_ref.shape[1]
    cols = pl.ds(pl.multiple_of(pl.program_id(1) * tn, tn), tn)
    o_ref[...] = jnp.dot(xn_ref[...], w_ref[:, cols], preferred_element_type=F32)


def _inproj(h2d, gain, w, tm=512, tn=3072):
    m, d = h2d.shape
    n = w.shape[1]
    return pl.pallas_call(
        _inproj_kernel,
        grid=(m // tm, n // tn),
        in_specs=[
            pl.BlockSpec((tm, d), lambda i, j: (i, 0)),
            pl.BlockSpec((1, d), lambda i, j: (0, 0)),
            pl.BlockSpec((d, n), lambda i, j: (0, 0), pipeline_mode=pl.Buffered(1)),
        ],
        out_specs=pl.BlockSpec((tm, tn), lambda i, j: (i, j)),
        out_shape=jax.ShapeDtypeStruct((m, n), F32),
        scratch_shapes=[pltpu.VMEM((tm, d), BF16)],
        compiler_params=_cparams(("parallel", "arbitrary")),
        name="inproj",
    )(h2d, gain, w)


def _scan_kernel(reverse, layer, *refs):
    if reverse:
        (gq, gk, gv, ga, hq, hf, hi, wg, gb, lbl, tri,
         out, qs, ks, gs, st) = refs
    else:
        (gq, gk, gv, gg, ga, hq, hf, hi, hg, ob, wg, gb, lbl, tri, onorm,
         out, qs, ks, gs, st, osum) = refs
    tb = qs.shape[0]
    n_chunks = tb // SCAN_CHUNK
    c = SCAN_CHUNK

    @pl.when(pl.program_id(1) == 0)
    def _():
        st[...] = jnp.zeros_like(st)

    qs[:, :GLA_W] = gq[...] * (GLA_DK ** -0.5)
    hqv = hq[...]
    qs[:, GLA_W:] = hqv * _sigmoid(hqv)
    ga_hi, ga_lo = _split_bf16(ga[...], 2)
    wg_hi, wg_lo = _split_bf16(wg[...], 2)
    z = (jnp.dot(ga_hi, wg_hi, preferred_element_type=F32)
         + jnp.dot(ga_lo, wg_hi, preferred_element_type=F32)
         + jnp.dot(ga_hi, wg_lo, preferred_element_type=F32)) + gb[...]
    lg_gla = (jnp.minimum(z, 0.0) - _softplus_neg_abs(z)[1]) * (1.0 / GLA_GATE_TEMP)
    logits = lbl[...]
    e = jnp.exp(logits - jnp.max(logits, axis=0, keepdims=True))
    sm = e / jnp.sum(e, axis=0, keepdims=True)
    lb = jnp.zeros_like(sm[0:1])
    for dd in range(1, layer + 1):
        lb = lb + sm[dd:dd + 1]
    zf = hf[...]
    t, sp = _softplus_neg_abs(zf)
    ks[:, :GLA_W] = gk[...]
    ks[:, GLA_W:] = (1.0 - lb) * (jnp.where(zf > 0.0, t, 1.0) / (1.0 + t))
    a = jnp.log(lb)
    b = jnp.log1p(-lb) + (jnp.minimum(zf, 0.0) - sp)
    lg_h = jnp.maximum(a, b) + _softplus_neg_abs(a - b)[1]
    tri_b = tri[...]
    gs[:, :GLA_W] = sum(jnp.dot(tri_b, part, preferred_element_type=F32)
                        for part in _split_bf16(lg_gla, 3))
    gs[:, GLA_W:] = sum(jnp.dot(tri_b, part, preferred_element_type=F32)
                        for part in _split_bf16(lg_h, 3))

    row = lax.broadcasted_iota(jnp.int32, (c, LANES), 0)
    ones = jnp.ones((LANES, LANES), BF16)
    ii = lax.broadcasted_iota(jnp.int32, (c, c), 0)
    jj = lax.broadcasted_iota(jnp.int32, (c, c), 1)
    live_cc = (ii <= jj) if reverse else (ii >= jj)
    o_dst = out if reverse else osum
    nt_dims = (((1,), (1,)), ((), ()))

    def intra_direct(q, k, g, v):
        parts = []
        for j in range(c):
            live = (row <= j) if reverse else (row >= j)
            dec = jnp.exp(jnp.where(live, g - g[j:j + 1], -jnp.inf))
            parts.append((q * dec * k[j:j + 1]).astype(BF16))
        att = jnp.dot(jnp.concatenate(parts, axis=0), ones,
                      preferred_element_type=F32)
        o_intra = att[0:c] * v[0:1]
        for j in range(1, c):
            o_intra = o_intra + att[j * c:(j + 1) * c] * v[j:j + 1]
        return o_intra

    tn_dims = (((0,), (0,)), ((), ()))

    def values(h, rs):
        if h < GLA_HEADS:
            return gv[rs, h * LANES:(h + 1) * LANES]
        return hi[rs, (h - GLA_HEADS) * LANES:(h - GLA_HEADS + 1) * LANES]

    def block_factored():
        order = range(n_chunks - 1, -1, -1) if reverse else range(n_chunks)
        for h in range(SCAN_HEADS):
            hl = slice(h * LANES, (h + 1) * LANES)
            staged = []
            for cidx in order:
                rs = slice(cidx * c, (cidx + 1) * c)
                q, k, g = qs[rs, hl], ks[rs, hl], gs[rs, hl]
                v = values(h, rs).astype(BF16)
                g_near = g[c - 1:c] if reverse else g[0:1]
                g_far = g[0:1] if reverse else g[c - 1:c]
                att = lax.dot_general((q * jnp.exp(g - g_near)).astype(BF16),
                                      (k * jnp.exp(g_near - g)).astype(BF16),
                                      nt_dims, preferred_element_type=F32)
                kv = lax.dot_general(v, (k * jnp.exp(g_far - g)).astype(BF16),
                                     tn_dims, preferred_element_type=F32)
                staged.append((rs, (q * jnp.exp(g)).astype(BF16), att, v, jnp.exp(g_far), kv))
            s_t = st[h]
            for rs, q_dec, att, v, decay, kv in staged:
                o_inter = lax.dot_general(q_dec, s_t.astype(BF16), nt_dims,
                                          preferred_element_type=F32)
                o_intra = jnp.dot(jnp.where(live_cc, att, 0.0).astype(BF16), v,
                                  preferred_element_type=F32)
                o_dst[rs, hl] = o_inter + o_intra
                s_t = s_t * decay + kv
            st[h] = s_t

    def chunk_direct(ci, carry):
        cidx = (n_chunks - 1 - ci) if reverse else ci
        rs = pl.ds(pl.multiple_of(cidx * c, c), c)
        for h in range(SCAN_HEADS):
            hl = slice(h * LANES, (h + 1) * LANES)
            q, k, g = qs[rs, hl], ks[rs, hl], gs[rs, hl]
            v = values(h, rs)
            g_far = g[0:1] if reverse else g[c - 1:c]
            s_t = st[h]
            o_inter = lax.dot_general((q * jnp.exp(g)).astype(BF16), s_t.astype(BF16),
                                      nt_dims, preferred_element_type=F32)
            o_dst[rs, hl] = o_inter + intra_direct(q, k, g, v)
            st[h] = s_t * jnp.exp(g_far) + lax.dot_general(
                v.astype(BF16), (k * jnp.exp(g_far - g)).astype(BF16), tn_dims,
                preferred_element_type=F32)
        return carry

    bounded = jnp.min(gs[...]) >= -SCAN_FACTOR_LIMIT
    pl.when(bounded)(block_factored)

    @pl.when(jnp.logical_not(bounded))
    def _():
        lax.fori_loop(0, n_chunks, chunk_direct, 0)

    if not reverse:
        for h in range(SCAN_HEADS):
            hl = slice(h * LANES, (h + 1) * LANES)
            o = osum[:, hl] + ob[:, hl]
            if h < GLA_HEADS:
                gate = gg[:, hl]
            else:
                gate = hg[:, (h - GLA_HEADS) * LANES:(h - GLA_HEADS + 1) * LANES]
            y = _rms(o, onorm[:, hl]) * (gate * _sigmoid(gate))
            out[:, hl] = y.astype(out.dtype)


def _scan(u, o_b, wg, gb, lbl, tri, onorm, *, batch, seq, layer, reverse):
    tb = SCAN_BLOCK
    nt = seq // tb

    def rows(b, t):
        return b * nt + ((nt - 1 - t) if reverse else t)

    def ublock(cb):
        return pl.BlockSpec((tb, 512), lambda b, t: (rows(b, t), cb))

    def const(shape):
        return pl.BlockSpec(shape, lambda b, t: (0,) * len(shape))

    ga_spec = pl.BlockSpec((tb, LANES), lambda b, t: (rows(b, t), CB128_GA))
    wide = pl.BlockSpec((tb, SCAN_WIDTH), lambda b, t: (rows(b, t), 0))
    params = [const((LANES, GLA_W)), const((1, GLA_W)), const(lbl.shape), const((tb, tb))]
    scratch = [pltpu.VMEM((tb, SCAN_WIDTH), F32)] * 3 + [pltpu.VMEM((SCAN_HEADS, LANES, LANES), F32)]
    if reverse:
        in_specs = [ublock(CB_GQ), ublock(CB_GK), ublock(CB_GV), ga_spec,
                    ublock(CB_HQ), ublock(CB_HFB), ublock(CB_HI)] + params
        args = [u, u, u, u, u, u, u, wg, gb, lbl, tri]
        out_dtype = F32
    else:
        in_specs = [ublock(CB_GQ), ublock(CB_GK), ublock(CB_GV), ublock(CB_GG), ga_spec,
                    ublock(CB_HQ), ublock(CB_HFF), ublock(CB_HI), ublock(CB_HG), wide] + params + [
                        const((1, SCAN_WIDTH))]
        args = [u, u, u, u, u, u, u, u, u, o_b, wg, gb, lbl, tri, onorm]
        out_dtype = BF16
        scratch = scratch + [pltpu.VMEM((tb, SCAN_WIDTH), F32)]
    return pl.pallas_call(
        functools.partial(_scan_kernel, reverse, layer),
        grid=(batch, nt),
        in_specs=in_specs,
        out_specs=wide,
        out_shape=jax.ShapeDtypeStruct((batch * seq, SCAN_WIDTH), out_dtype),
        scratch_shapes=scratch,
        compiler_params=_cparams(("arbitrary", "arbitrary")),
        name="scan_bwd" if reverse else "scan_fwd",
    )(*args)


ATTN_DQK_PAD = 256
ATTN_STAB_ROW = MLA_DQK


def _mla_prep_kernel(cq_ref, ckv_ref, kr_ref, cs_ref, gq_ref, gkv_ref, wq_ref, wkv_ref,
                     qt_out, ka_out, vt_out):
    scale = MLA_DQK ** -0.5 * float(np.log2(np.e))
    tm = cq_ref.shape[0]
    cs = cs_ref[...]
    yq = jnp.dot(_rms(cq_ref[...], gq_ref[...]).astype(BF16), wq_ref[...],
                 preferred_element_type=F32)
    ykv = jnp.dot(_rms(ckv_ref[...], gkv_ref[...]).astype(BF16), wkv_ref[...],
                  preferred_element_type=F32)
    lane = lax.broadcasted_iota(jnp.int32, (tm, LANES), 1)
    is_rope = lane < MLA_ROPE

    def rotary(t):
        t = t * cs
        return t + pltpu.roll(t, MLA_ROPE, 1)

    k_hi = jnp.where(is_rope, rotary(kr_ref[...]),
                     (lane == MLA_ROPE).astype(F32)).astype(BF16)
    for h in range(MLA_HEADS):
        yh = yq[:, h * 256:(h + 1) * 256]
        q_hi = jnp.where(is_rope, rotary(yh[:, MLA_NOPE:]), 0.0)
        q_full = jnp.concatenate([yh[:, :MLA_NOPE], q_hi], axis=1) * scale
        qt_out[h] = q_full.T.astype(BF16)
        kvh = ykv[:, h * 256:(h + 1) * 256]
        ka_out[h, :, :MLA_NOPE] = kvh[:, :MLA_NOPE].astype(BF16)
        ka_out[h, :, MLA_NOPE:] = k_hi
        vt_out[h] = kvh[:, MLA_NOPE:].T.astype(BF16)


def _mla_prep(u, cs, gq, gkv, wq, wkv, *, batch, seq, tm=512):
    nt = seq // tm

    def rows(b, t):
        return b * nt + t

    def const(shape):
        return pl.BlockSpec(shape, lambda b, t: (0,) * len(shape))

    return pl.pallas_call(
        _mla_prep_kernel,
        grid=(batch, nt),
        in_specs=[
            pl.BlockSpec((tm, 512), lambda b, t: (rows(b, t), CB_CQ)),
            pl.BlockSpec((tm, 512), lambda b, t: (rows(b, t), CB_CKV)),
            pl.BlockSpec((tm, LANES), lambda b, t: (rows(b, t), CB128_KR)),
            pl.BlockSpec((tm, LANES), lambda b, t: (rows(b, t), 0)),
            const((1, MLA_Q_RANK)), const((1, MLA_KV_RANK)),
            const(wq.shape), const(wkv.shape),
        ],
        out_specs=[
            pl.BlockSpec((None, MLA_HEADS, ATTN_DQK_PAD, tm), lambda b, t: (b, 0, 0, t)),
            pl.BlockSpec((None, MLA_HEADS, tm, ATTN_DQK_PAD), lambda b, t: (b, 0, t, 0)),
            pl.BlockSpec((None, MLA_HEADS, MLA_DV, tm), lambda b, t: (b, 0, 0, t)),
        ],
        out_shape=[
            jax.ShapeDtypeStruct((batch, MLA_HEADS, ATTN_DQK_PAD, seq), BF16),
            jax.ShapeDtypeStruct((batch, MLA_HEADS, seq, ATTN_DQK_PAD), BF16),
            jax.ShapeDtypeStruct((batch, MLA_HEADS, MLA_DV, seq), BF16),
        ],
        compiler_params=_cparams(("parallel", "parallel")),
        name="mla_prep",
    )(u, u, u, cs, gq, gkv, wq, wkv)


ATTN_HEADROOM = 32.0


def _attn_kernel(tk, qt_ref, ka_ref, vt_ref, o_ref, qa_ref, m_ref, acc_ref, l_ref,
                 p0_ref, p1_ref, ls0_ref, ls1_ref):
    tq = qt_ref.shape[1]
    n_chunks = ka_ref.shape[0] // tk
    t = MXU_TILE
    sub = l_ref.shape[0]
    stab_rows = slice(ATTN_STAB_ROW, ATTN_STAB_ROW + 16)
    row0 = lax.broadcasted_iota(jnp.int32, (16, tq), 0) == 0

    def set_stabiliser(m):
        m_ref[...] = m
        qa_ref[stab_rows, :] = jnp.where(row0, -m, 0.0).astype(BF16)

    def bf16_round(x):
        return x.astype(BF16).astype(F32)

    def probabilities(c, p_ref, ls_ref):
        start = pl.multiple_of(c * tk, tk)
        mx = None
        for qh in range(tq // t):
            ql = slice(qh * t, (qh + 1) * t)
            mq, ls = None, None
            for sl in range(tk // t):
                s = jnp.dot(ka_ref[pl.ds(start + sl * t, t), :], qa_ref[:, ql],
                            preferred_element_type=F32)
                cur = jnp.max(s, axis=0, keepdims=True)
                mq = cur if mq is None else jnp.maximum(mq, cur)
                p = jnp.exp2(s)
                part = jnp.sum(p.reshape(t // sub, sub, t), axis=0)
                ls = part if ls is None else ls + part
                p_ref[sl * t:(sl + 1) * t, ql] = p.astype(BF16)
            ls_ref[:, ql] = ls
            mx = mq if mx is None else jnp.concatenate([mx, mq], axis=1)
        return mx

    def accumulate(c, p_ref, ls_ref):
        start = pl.multiple_of(c * tk, tk)
        for qh in range(tq // t):
            ql = slice(qh * t, (qh + 1) * t)
            pv = None
            for sl in range(tk // t):
                d = jnp.dot(vt_ref[:, pl.ds(start + sl * t, t)], p_ref[sl * t:(sl + 1) * t, ql],
                            preferred_element_type=F32)
                pv = d if pv is None else pv + d
            acc_ref[:, ql] += pv
        l_ref[...] += ls_ref[...]

    def checked_probabilities(c, p_ref, ls_ref, first=False):
        mx = probabilities(c, p_ref, ls_ref)
        bad = jnp.max(mx) > ATTN_HEADROOM
        if first:
            bad = jnp.logical_or(bad, jnp.min(mx) < -ATTN_HEADROOM)

        @pl.when(bad)
        def _():
            m_old = m_ref[...]
            m_new = bf16_round(m_old + (mx if first else jnp.maximum(mx, 0.0)))
            if not first:
                shrink = jnp.exp2(m_old - m_new)
                acc_ref[...] *= shrink
                l_ref[...] *= shrink
            set_stabiliser(m_new)
            probabilities(c, p_ref, ls_ref)

    qa_ref[...] = qt_ref[...]
    m_ref[...] = jnp.zeros_like(m_ref)
    acc_ref[...] = jnp.zeros_like(acc_ref)
    l_ref[...] = jnp.zeros_like(l_ref)

    even, odd = (p0_ref, ls0_ref), (p1_ref, ls1_ref)
    checked_probabilities(0, *even, first=True)

    def pair(c, last):
        accumulate(c, *even)
        checked_probabilities(c + 1, *odd)
        accumulate(c + 1, *odd)
        if not last:
            checked_probabilities(c + 2, *even)

    def body(j, carry):
        pair(2 * j, last=False)
        return carry

    lax.fori_loop(0, n_chunks // 2 - 1, body, 0)
    pair(n_chunks - 2, last=True)
    denom = jnp.sum(l_ref[...], axis=0, keepdims=True)
    o_ref[...] = (acc_ref[...] / denom).T.astype(o_ref.dtype)


def _attention(qt, ka, vt, *, tq=2048, tk=2048):
    batch, heads, dqk, seq = qt.shape
    dv = vt.shape[2]
    assert seq % tq == 0 and seq % (2 * tk) == 0, "key chunks are consumed in pairs"
    return pl.pallas_call(
        functools.partial(_attn_kernel, tk),
        grid=(batch, heads, seq // tq),
        in_specs=[
            pl.BlockSpec((None, None, dqk, tq), lambda b, h, i: (b, h, 0, i)),
            pl.BlockSpec((None, None, seq, dqk), lambda b, h, i: (b, h, 0, 0)),
            pl.BlockSpec((None, None, dv, seq), lambda b, h, i: (b, h, 0, 0)),
        ],
        out_specs=pl.BlockSpec((None, tq, MLA_DV), lambda b, h, i: (b, i, h)),
        out_shape=jax.ShapeDtypeStruct((batch, seq, heads * MLA_DV), BF16),
        scratch_shapes=[pltpu.VMEM((dqk, tq), BF16), pltpu.VMEM((1, tq), F32),
                        pltpu.VMEM((dv, tq), F32), pltpu.VMEM((SUBLANES, tq), F32),
                        pltpu.VMEM((tk, tq), BF16), pltpu.VMEM((tk, tq), BF16),
                        pltpu.VMEM((SUBLANES, tq), F32), pltpu.VMEM((SUBLANES, tq), F32)],
        compiler_params=_cparams(("parallel", "parallel", "arbitrary")),
        name="mla_attn",
    )(qt, ka, vt)


def _outproj_kernel(h_ref, ys_ref, ym_ref, w_ref, o_ref):
    ws = ys_ref.shape[1]
    o_ref[...] = (h_ref[...]
                  + jnp.dot(ys_ref[...], w_ref[:ws, :], preferred_element_type=F32)
                  + jnp.dot(ym_ref[...], w_ref[ws:, :], preferred_element_type=F32))


def _outproj(h2d, ys, ym, w, tm=512):
    m, d = h2d.shape
    return pl.pallas_call(
        _outproj_kernel,
        grid=(m // tm,),
        in_specs=[
            pl.BlockSpec((tm, d), lambda i: (i, 0)),
            pl.BlockSpec((tm, ys.shape[1]), lambda i: (i, 0)),
            pl.BlockSpec((tm, ym.shape[1]), lambda i: (i, 0)),
            pl.BlockSpec(w.shape, lambda i: (0, 0)),
        ],
        out_specs=pl.BlockSpec((tm, d), lambda i: (i, 0)),
        out_shape=jax.ShapeDtypeStruct((m, d), F32),
        compiler_params=_cparams(("parallel",)),
        name="outproj",
    )(h2d, ys, ym, w)


def _ffn_kernel(final, h_ref, g_ref, gf_ref, w1_ref, w3_ref, w2_ref, o_ref, xn_ref, acc_ref):
    j = pl.program_id(1)

    @pl.when(j == 0)
    def _():
        xn_ref[...] = _rms(h_ref[...], g_ref[...]).astype(BF16)
        acc_ref[...] = jnp.zeros_like(acc_ref)

    xn = xn_ref[...]
    a = jnp.dot(xn, w1_ref[...], preferred_element_type=F32)
    b = jnp.dot(xn, w3_ref[...], preferred_element_type=F32)
    gated = (a * _sigmoid(a) * b).astype(BF16)
    acc_ref[...] += jnp.dot(gated, w2_ref[...], preferred_element_type=F32)

    @pl.when(j == pl.num_programs(1) - 1)
    def _():
        y = h_ref[...] + acc_ref[...]
        if final:
            y = _rms(y, gf_ref[...])
        o_ref[...] = y


def _ffn(h2d, gain, gain_final, w1, w3, w2, *, final, tm=512, tf=512):
    m, d = h2d.shape
    f = w1.shape[1]
    return pl.pallas_call(
        functools.partial(_ffn_kernel, final),
        grid=(m // tm, f // tf),
        in_specs=[
            pl.BlockSpec((tm, d), lambda i, j: (i, 0)),
            pl.BlockSpec((1, d), lambda i, j: (0, 0)),
            pl.BlockSpec((1, d), lambda i, j: (0, 0)),
            pl.BlockSpec((d, tf), lambda i, j: (0, j)),
            pl.BlockSpec((d, tf), lambda i, j: (0, j)),
            pl.BlockSpec((tf, d), lambda i, j: (j, 0)),
        ],
        out_specs=pl.BlockSpec((tm, d), lambda i, j: (i, 0)),
        out_shape=jax.ShapeDtypeStruct((m, d), F32),
        scratch_shapes=[pltpu.VMEM((tm, d), BF16), pltpu.VMEM((tm, d), F32)],
        compiler_params=_cparams(("parallel", "arbitrary")),
        name="ffn",
    )(h2d, gain, gain_final, w1, w3, w2)


def _pad_heads(w, heads, dk):
    lead = w.shape[:-1]
    w = w.reshape(lead + (heads, dk))
    w = jnp.pad(w, [(0, 0)] * len(lead) + [(0, 0), (0, LANES - dk)])
    return w.reshape(lead + (heads * LANES,))


def _layout_w_in(w):
    w = w.astype(BF16)
    sizes = (256, 256, 512, 16, 16, 512, 512, 512, 512, 512, 512, 512, 512, 64)
    (gq, gk, gv, gaf, gab, gg, hq, hff, hfb, hi, hg, cq, ckv, kr) = jnp.split(
        w, np.cumsum(sizes)[:-1].tolist(), axis=1)
    half = MLA_ROPE // 2
    kr_swapped = jnp.concatenate([kr[:, half:], kr[:, :half]], axis=1)
    d = w.shape[0]
    cols = [_pad_heads(gq, GLA_HEADS, GLA_DK), _pad_heads(gk, GLA_HEADS, GLA_DK), gv, gg,
            hq, hff, hfb, hi, hg, cq, ckv, kr, kr_swapped,
            gaf, gab, jnp.zeros((d, LANES - 2 * GLA_GATE_RANK), w.dtype)]
    out = jnp.concatenate(cols, axis=1)
    return jnp.pad(out, [(0, 0), (0, U_WIDTH - out.shape[1])])


def _layout_w_qb(w):
    r = w.shape[0]
    w = w.astype(BF16).reshape(r, MLA_HEADS, MLA_DQK)
    rope = w[:, :, MLA_NOPE:]
    half = MLA_ROPE // 2
    swapped = jnp.concatenate([rope[:, :, half:], rope[:, :, :half]], axis=2)
    return jnp.concatenate([w, swapped], axis=2).reshape(r, MLA_HEADS * 256)


def _tri(tb, reverse):
    r = np.arange(tb)[:, None]
    c = np.arange(tb)[None, :]
    same = (r // SCAN_CHUNK) == (c // SCAN_CHUNK)
    keep = (c >= r) if reverse else (c <= r)
    return jnp.asarray((same & keep).astype(np.float32)).astype(BF16)


def kernel(x, positions, w_in, gla_gate_up, gla_gate_bias, gla_out_norm, hgrn_lb_logits,
           hgrn_out_norm, mla_qa_norm, mla_w_qb, mla_kva_norm, mla_w_kvb, w_out,
           norm_mix, norm_ffn, ffn_w1, ffn_w3, ffn_w2, norm_final):
    batch, seq, d = x.shape
    depth = w_in.shape[0]
    m = batch * seq

    inv_freq = 1.0 / (ROPE_THETA ** (jnp.arange(0, MLA_ROPE, 2, dtype=F32) / MLA_ROPE))
    ang = positions.astype(F32)[..., None] * inv_freq
    cos, sin = jnp.cos(ang), jnp.sin(ang)
    cs = jnp.concatenate([cos, cos, -sin, sin], axis=-1).reshape(m, LANES)

    tri_f, tri_b = _tri(SCAN_BLOCK, False), _tri(SCAN_BLOCK, True)
    h = x.reshape(m, d)
    for l in range(depth):
        u = _inproj(h, norm_mix[l].reshape(1, d), _layout_w_in(w_in[l]))

        def gate_w(direction):
            wgd = _pad_heads(gla_gate_up[l, direction], GLA_HEADS, GLA_DK)
            lo = direction * GLA_GATE_RANK
            return jnp.pad(wgd, [(lo, LANES - lo - GLA_GATE_RANK), (0, 0)])

        gb = _pad_heads(gla_gate_bias[l], GLA_HEADS, GLA_DK)
        onorm = jnp.concatenate([jnp.tile(gla_out_norm[l], GLA_HEADS),
                                 jnp.tile(hgrn_out_norm[l], HGRN_HEADS)]).reshape(1, SCAN_WIDTH)
        o_b = _scan(u, None, gate_w(1), gb[1:2], hgrn_lb_logits[1], tri_b, None,
                    batch=batch, seq=seq, layer=l, reverse=True)
        y_scan = _scan(u, o_b, gate_w(0), gb[0:1], hgrn_lb_logits[0], tri_f, onorm,
                       batch=batch, seq=seq, layer=l, reverse=False)

        q, k, vt = _mla_prep(u, cs, mla_qa_norm[l].reshape(1, -1), mla_kva_norm[l].reshape(1, -1),
                             _layout_w_qb(mla_w_qb[l]), mla_w_kvb[l].astype(BF16),
                             batch=batch, seq=seq)
        y_mla = _attention(q, k, vt).reshape(m, MLA_HEADS * MLA_DV)

        h = _outproj(h, y_scan, y_mla, w_out[l].astype(BF16))
        h = _ffn(h, norm_ffn[l].reshape(1, d), norm_final.reshape(1, d),
                 ffn_w1[l].astype(BF16), ffn_w3[l].astype(BF16), ffn_w2[l].astype(BF16),
                 final=(l == depth - 1))
    return h.reshape(batch, seq, d)
```

```python
import functools

import jax
import jax.numpy as jnp
import numpy as np
from jax import lax
from jax.experimental import pallas as pl
from jax.experimental.pallas import tpu as pltpu

D_MODEL = 2048
GLA_HEADS, GLA_DK, GLA_DV = 4, 64, 128
GLA_GATE_RANK = 16
GLA_GATE_TEMP = 16.0
HGRN_HEADS, HGRN_DK, HGRN_DV = 4, 128, 128
MLA_HEADS, MLA_Q_RANK, MLA_KV_RANK = 8, 512, 512
MLA_NOPE, MLA_ROPE, MLA_DV = 128, 64, 128
MLA_DQK = MLA_NOPE + MLA_ROPE
ROPE_THETA = 10000.0
EPS = 1e-6

LANES = 128
SUBLANES = 8
MXU_TILE = 256
SCAN_HEADS = GLA_HEADS + HGRN_HEADS
SCAN_WIDTH = SCAN_HEADS * LANES
GLA_W = GLA_HEADS * LANES
HGRN_W = HGRN_HEADS * LANES

CB_GQ, CB_GK, CB_GV, CB_GG, CB_HQ, CB_HFF, CB_HFB, CB_HI, CB_HG, CB_CQ, CB_CKV = range(11)
CB128_KR = 44
CB128_GA = 45
U_WIDTH = 5888

SCAN_CHUNK = 32
SCAN_BLOCK = 512
SCAN_FACTOR_LIMIT = 60.0
VMEM_LIMIT = 56 * 1024 * 1024

F32 = jnp.float32
BF16 = jnp.bfloat16


def _cparams(sem):
    return pltpu.CompilerParams(dimension_semantics=sem, vmem_limit_bytes=VMEM_LIMIT)


def _rms(x, gain):
    return x * lax.rsqrt(jnp.mean(x * x, axis=-1, keepdims=True) + EPS) * gain


def _sigmoid(z):
    return 1.0 / (1.0 + jnp.exp(-z))


def _softplus_neg_abs(z):
    t = jnp.exp(-jnp.abs(z))
    return t, jnp.log(1.0 + t)


def _split_bf16(x, n):
    parts = []
    for _ in range(n):
        p = x.astype(BF16)
        parts.append(p)
        x = x - p.astype(F32)
    return parts


def _inproj_kernel(x_ref, g_ref, w_ref, o_ref, xn_ref):
    @pl.when(pl.program_id(1) == 0)
    def _():
        xn_ref[...] = _rms(x_ref[...], g_ref[...]).astype(BF16)

    tn = o_ref.shape[1]
    cols = pl.ds(pl.multiple_of(pl.program_id(1) * tn, tn), tn)
    o_ref[...] = jnp.dot(xn_ref[...], w_ref[:, cols], preferred_element_type=F32)


def _inproj(h2d, gain, w, tm=512, tn=U_WIDTH // 2):
    m, d = h2d.shape
    n = w.shape[1]
    return pl.pallas_call(
        _inproj_kernel,
        grid=(m // tm, n // tn),
        in_specs=[
            pl.BlockSpec((tm, d), lambda i, j: (i, 0)),
            pl.BlockSpec((1, d), lambda i, j: (0, 0)),
            pl.BlockSpec((d, n), lambda i, j: (0, 0), pipeline_mode=pl.Buffered(1)),
        ],
        out_specs=pl.BlockSpec((tm, tn), lambda i, j: (i, j)),
        out_shape=jax.ShapeDtypeStruct((m, n), F32),
        scratch_shapes=[pltpu.VMEM((tm, d), BF16)],
        compiler_params=_cparams(("parallel", "arbitrary")),
        name="inproj",
    )(h2d, gain, w)


def _scan_kernel(reverse, layer, *refs):
    if reverse:
        (gq, gk, gv, ga, hq, hf, hi, wg, gb, lbl, tri,
         out, qs, ks, gs, st) = refs
    else:
        (gq, gk, gv, gg, ga, hq, hf, hi, hg, ob, wg, gb, lbl, tri, onorm,
         out, qs, ks, gs, st, osum) = refs
    tb = qs.shape[0]
    n_chunks = tb // SCAN_CHUNK
    c = SCAN_CHUNK

    @pl.when(pl.program_id(1) == 0)
    def _():
        st[...] = jnp.zeros_like(st)

    qs[:, :GLA_W] = gq[...] * (GLA_DK ** -0.5)
    hqv = hq[...]
    qs[:, GLA_W:] = hqv * _sigmoid(hqv)
    ga_hi, ga_lo = _split_bf16(ga[...], 2)
    wg_hi, wg_lo = _split_bf16(wg[...], 2)
    z = (jnp.dot(ga_hi, wg_hi, preferred_element_type=F32)
         + jnp.dot(ga_lo, wg_hi, preferred_element_type=F32)
         + jnp.dot(ga_hi, wg_lo, preferred_element_type=F32)) + gb[...]
    lg_gla = (jnp.minimum(z, 0.0) - _softplus_neg_abs(z)[1]) * (1.0 / GLA_GATE_TEMP)
    logits = lbl[...]
    e = jnp.exp(logits - jnp.max(logits, axis=0, keepdims=True))
    sm = e / jnp.sum(e, axis=0, keepdims=True)
    lb = jnp.zeros_like(sm[0:1])
    for dd in range(1, layer + 1):
        lb = lb + sm[dd:dd + 1]
    zf = hf[...]
    t, sp = _softplus_neg_abs(zf)
    ks[:, :GLA_W] = gk[...]
    ks[:, GLA_W:] = (1.0 - lb) * (jnp.where(zf > 0.0, t, 1.0) / (1.0 + t))
    a = jnp.log(lb)
    b = jnp.log1p(-lb) + (jnp.minimum(zf, 0.0) - sp)
    lg_h = jnp.maximum(a, b) + _softplus_neg_abs(a - b)[1]
    tri_b = tri[...]
    gs[:, :GLA_W] = sum(jnp.dot(tri_b, part, preferred_element_type=F32)
                        for part in _split_bf16(lg_gla, 3))
    gs[:, GLA_W:] = sum(jnp.dot(tri_b, part, preferred_element_type=F32)
                        for part in _split_bf16(lg_h, 3))

    row = lax.broadcasted_iota(jnp.int32, (c, LANES), 0)
    ones = jnp.ones((LANES, LANES), BF16)
    ii = lax.broadcasted_iota(jnp.int32, (c, c), 0)
    jj = lax.broadcasted_iota(jnp.int32, (c, c), 1)
    live_cc = (ii <= jj) if reverse else (ii >= jj)
    o_dst = out if reverse else osum
    nt_dims = (((1,), (1,)), ((), ()))

    def intra_direct(q, k, g, v):
        parts = []
        for j in range(c):
            live = (row <= j) if reverse else (row >= j)
            dec = jnp.exp(jnp.where(live, g - g[j:j + 1], -jnp.inf))
            parts.append((q * dec * k[j:j + 1]).astype(BF16))
        att = jnp.dot(jnp.concatenate(parts, axis=0), ones,
                      preferred_element_type=F32)
        o_intra = att[0:c] * v[0:1]
        for j in range(1, c):
            o_intra = o_intra + att[j * c:(j + 1) * c] * v[j:j + 1]
        return o_intra

    tn_dims = (((0,), (0,)), ((), ()))

    def values(h, rs):
        if h < GLA_HEADS:
            return gv[rs, h * LANES:(h + 1) * LANES]
        return hi[rs, (h - GLA_HEADS) * LANES:(h - GLA_HEADS + 1) * LANES]

    def block_factored():
        order = range(n_chunks - 1, -1, -1) if reverse else range(n_chunks)
        for h in range(SCAN_HEADS):
            hl = slice(h * LANES, (h + 1) * LANES)
            staged = []
            for cidx in order:
                rs = slice(cidx * c, (cidx + 1) * c)
                q, k, g = qs[rs, hl], ks[rs, hl], gs[rs, hl]
                v = values(h, rs).astype(BF16)
                g_near = g[c - 1:c] if reverse else g[0:1]
                g_far = g[0:1] if reverse else g[c - 1:c]
                att = lax.dot_general((q * jnp.exp(g - g_near)).astype(BF16),
                                      (k * jnp.exp(g_near - g)).astype(BF16),
                                      nt_dims, preferred_element_type=F32)
                kv = lax.dot_general(v, (k * jnp.exp(g_far - g)).astype(BF16),
                                     tn_dims, preferred_element_type=F32)
                staged.append((rs, (q * jnp.exp(g)).astype(BF16), att, v, jnp.exp(g_far), kv))
            s_t = st[h]
            for rs, q_dec, att, v, decay, kv in staged:
                o_inter = lax.dot_general(q_dec, s_t.astype(BF16), nt_dims,
                                          preferred_element_type=F32)
                o_intra = jnp.dot(jnp.where(live_cc, att, 0.0).astype(BF16), v,
                                  preferred_element_type=F32)
                o_dst[rs, hl] = o_inter + o_intra
                s_t = s_t * decay + kv
            st[h] = s_t

    def chunk_direct(ci, carry):
        cidx = (n_chunks - 1 - ci) if reverse else ci
        rs = pl.ds(pl.multiple_of(cidx * c, c), c)
        for h in range(SCAN_HEADS):
            hl = slice(h * LANES, (h + 1) * LANES)
            q, k, g = qs[rs, hl], ks[rs, hl], gs[rs, hl]
            v = values(h, rs)
            g_far = g[0:1] if reverse else g[c - 1:c]
            s_t = st[h]
            o_inter = lax.dot_general((q * jnp.exp(g)).astype(BF16), s_t.astype(BF16),
                                      nt_dims, preferred_element_type=F32)
            o_dst[rs, hl] = o_inter + intra_direct(q, k, g, v)
            st[h] = s_t * jnp.exp(g_far) + lax.dot_general(
                v.astype(BF16), (k * jnp.exp(g_far - g)).astype(BF16), tn_dims,
                preferred_element_type=F32)
        return carry

    bounded = jnp.min(gs[...]) >= -SCAN_FACTOR_LIMIT
    pl.when(bounded)(block_factored)

    @pl.when(jnp.logical_not(bounded))
    def _():
        lax.fori_loop(0, n_chunks, chunk_direct, 0)

    if not reverse:
        for h in range(SCAN_HEADS):
            hl = slice(h * LANES, (h + 1) * LANES)
            o = osum[:, hl] + ob[:, hl]
            if h < GLA_HEADS:
                gate = gg[:, hl]
            else:
                gate = hg[:, (h - GLA_HEADS) * LANES:(h - GLA_HEADS + 1) * LANES]
            y = _rms(o, onorm[:, hl]) * (gate * _sigmoid(gate))
            out[:, hl] = y.astype(out.dtype)


def _scan(u, o_b, wg, gb, lbl, tri, onorm, *, batch, seq, layer, reverse):
    tb = SCAN_BLOCK
    nt = seq // tb

    def rows(b, t):
        return b * nt + ((nt - 1 - t) if reverse else t)

    def ublock(cb):
        return pl.BlockSpec((tb, 512), lambda b, t: (rows(b, t), cb))

    def const(shape):
        return pl.BlockSpec(shape, lambda b, t: (0,) * len(shape))

    ga_spec = pl.BlockSpec((tb, LANES), lambda b, t: (rows(b, t), CB128_GA))
    wide = pl.BlockSpec((tb, SCAN_WIDTH), lambda b, t: (rows(b, t), 0))
    params = [const((LANES, GLA_W)), const((1, GLA_W)), const(lbl.shape), const((tb, tb))]
    scratch = [pltpu.VMEM((tb, SCAN_WIDTH), F32)] * 3 + [pltpu.VMEM((SCAN_HEADS, LANES, LANES), F32)]
    if reverse:
        in_specs = [ublock(CB_GQ), ublock(CB_GK), ublock(CB_GV), ga_spec,
                    ublock(CB_HQ), ublock(CB_HFB), ublock(CB_HI)] + params
        args = [u, u, u, u, u, u, u, wg, gb, lbl, tri]
        out_dtype = F32
    else:
        in_specs = [ublock(CB_GQ), ublock(CB_GK), ublock(CB_GV), ublock(CB_GG), ga_spec,
                    ublock(CB_HQ), ublock(CB_HFF), ublock(CB_HI), ublock(CB_HG), wide] + params + [
                        const((1, SCAN_WIDTH))]
        args = [u, u, u, u, u, u, u, u, u, o_b, wg, gb, lbl, tri, onorm]
        out_dtype = BF16
        scratch = scratch + [pltpu.VMEM((tb, SCAN_WIDTH), F32)]
    return pl.pallas_call(
        functools.partial(_scan_kernel, reverse, layer),
        grid=(batch, nt),
        in_specs=in_specs,
        out_specs=wide,
        out_shape=jax.ShapeDtypeStruct((batch * seq, SCAN_WIDTH), out_dtype),
        scratch_shapes=scratch,
        compiler_params=_cparams(("arbitrary", "arbitrary")),
        name="scan_bwd" if reverse else "scan_fwd",
    )(*args)


ATTN_DQK_PAD = 256
ATTN_STAB_ROW = MLA_DQK


def _mla_prep_kernel(cq_ref, ckv_ref, kr_ref, cs_ref, gq_ref, gkv_ref, wq_ref, wkv_ref,
                     qt_out, ka_out, vt_out):
    scale = MLA_DQK ** -0.5 * float(np.log2(np.e))
    tm = cq_ref.shape[0]
    cs = cs_ref[...]
    yq = jnp.dot(_rms(cq_ref[...], gq_ref[...]).astype(BF16), wq_ref[...],
                 preferred_element_type=F32)
    ykv = jnp.dot(_rms(ckv_ref[...], gkv_ref[...]).astype(BF16), wkv_ref[...],
                  preferred_element_type=F32)
    lane = lax.broadcasted_iota(jnp.int32, (tm, LANES), 1)
    is_rope = lane < MLA_ROPE

    def rotary(t):
        t = t * cs
        return t + pltpu.roll(t, MLA_ROPE, 1)

    k_hi = jnp.where(is_rope, rotary(kr_ref[...]),
                     (lane == MLA_ROPE).astype(F32)).astype(BF16)
    for h in range(MLA_HEADS):
        yh = yq[:, h * 256:(h + 1) * 256]
        q_hi = jnp.where(is_rope, rotary(yh[:, MLA_NOPE:]), 0.0)
        q_full = jnp.concatenate([yh[:, :MLA_NOPE], q_hi], axis=1) * scale
        qt_out[h] = q_full.T.astype(BF16)
        kvh = ykv[:, h * 256:(h + 1) * 256]
        ka_out[h, :, :MLA_NOPE] = kvh[:, :MLA_NOPE].astype(BF16)
        ka_out[h, :, MLA_NOPE:] = k_hi
        vt_out[h] = kvh[:, MLA_NOPE:].T.astype(BF16)


def _mla_prep(u, cs, gq, gkv, wq, wkv, *, batch, seq, tm=512):
    nt = seq // tm

    def rows(b, t):
        return b * nt + t

    def const(shape):
        return pl.BlockSpec(shape, lambda b, t: (0,) * len(shape))

    return pl.pallas_call(
        _mla_prep_kernel,
        grid=(batch, nt),
        in_specs=[
            pl.BlockSpec((tm, 512), lambda b, t: (rows(b, t), CB_CQ)),
            pl.BlockSpec((tm, 512), lambda b, t: (rows(b, t), CB_CKV)),
            pl.BlockSpec((tm, LANES), lambda b, t: (rows(b, t), CB128_KR)),
            pl.BlockSpec((tm, LANES), lambda b, t: (rows(b, t), 0)),
            const((1, MLA_Q_RANK)), const((1, MLA_KV_RANK)),
            const(wq.shape), const(wkv.shape),
        ],
        out_specs=[
            pl.BlockSpec((None, MLA_HEADS, ATTN_DQK_PAD, tm), lambda b, t: (b, 0, 0, t)),
            pl.BlockSpec((None, MLA_HEADS, tm, ATTN_DQK_PAD), lambda b, t: (b, 0, t, 0)),
            pl.BlockSpec((None, MLA_HEADS, MLA_DV, tm), lambda b, t: (b, 0, 0, t)),
        ],
        out_shape=[
            jax.ShapeDtypeStruct((batch, MLA_HEADS, ATTN_DQK_PAD, seq), BF16),
            jax.ShapeDtypeStruct((batch, MLA_HEADS, seq, ATTN_DQK_PAD), BF16),
            jax.ShapeDtypeStruct((batch, MLA_HEADS, MLA_DV, seq), BF16),
        ],
        compiler_params=_cparams(("parallel", "parallel")),
        name="mla_prep",
    )(u, u, u, cs, gq, gkv, wq, wkv)


ATTN_HEADROOM = 32.0


def _attn_kernel(tk, qt_ref, ka_ref, vt_ref, o_ref, qa_ref, m_ref, acc_ref, l_ref,
                 p0_ref, p1_ref, ls0_ref, ls1_ref):
    tq = qt_ref.shape[1]
    n_chunks = ka_ref.shape[0] // tk
    t = MXU_TILE
    sub = l_ref.shape[0]
    stab_rows = slice(ATTN_STAB_ROW, ATTN_STAB_ROW + 16)
    row0 = lax.broadcasted_iota(jnp.int32, (16, tq), 0) == 0

    def set_stabiliser(m):
        m_ref[...] = m
        qa_ref[stab_rows, :] = jnp.where(row0, -m, 0.0).astype(BF16)

    def bf16_round(x):
        return x.astype(BF16).astype(F32)

    def probabilities(c, p_ref, ls_ref):
        start = pl.multiple_of(c * tk, tk)
        mx = None
        for qh in range(tq // t):
            ql = slice(qh * t, (qh + 1) * t)
            mq, ls = None, None
            for sl in range(tk // t):
                s = jnp.dot(ka_ref[pl.ds(start + sl * t, t), :], qa_ref[:, ql],
                            preferred_element_type=F32)
                cur = jnp.max(s, axis=0, keepdims=True)
                mq = cur if mq is None else jnp.maximum(mq, cur)
                p = jnp.exp2(s)
                part = jnp.sum(p.reshape(t // sub, sub, t), axis=0)
                ls = part if ls is None else ls + part
                p_ref[sl * t:(sl + 1) * t, ql] = p.astype(BF16)
            ls_ref[:, ql] = ls
            mx = mq if mx is None else jnp.concatenate([mx, mq], axis=1)
        return mx

    def accumulate(c, p_ref, ls_ref):
        start = pl.multiple_of(c * tk, tk)
        for qh in range(tq // t):
            ql = slice(qh * t, (qh + 1) * t)
            pv = None
            for sl in range(tk // t):
                d = jnp.dot(vt_ref[:, pl.ds(start + sl * t, t)], p_ref[sl * t:(sl + 1) * t, ql],
                            preferred_element_type=F32)
                pv = d if pv is None else pv + d
            acc_ref[:, ql] += pv
        l_ref[...] += ls_ref[...]

    def checked_probabilities(c, p_ref, ls_ref, first=False):
        mx = probabilities(c, p_ref, ls_ref)
        bad = jnp.max(mx) > ATTN_HEADROOM
        if first:
            bad = jnp.logical_or(bad, jnp.min(mx) < -ATTN_HEADROOM)

        @pl.when(bad)
        def _():
            m_old = m_ref[...]
            m_new = bf16_round(m_old + (mx if first else jnp.maximum(mx, 0.0)))
            if not first:
                shrink = jnp.exp2(m_old - m_new)
                acc_ref[...] *= shrink
                l_ref[...] *= shrink
            set_stabiliser(m_new)
            probabilities(c, p_ref, ls_ref)

    qa_ref[...] = qt_ref[...]
    m_ref[...] = jnp.zeros_like(m_ref)
    acc_ref[...] = jnp.zeros_like(acc_ref)
    l_ref[...] = jnp.zeros_like(l_ref)

    even, odd = (p0_ref, ls0_ref), (p1_ref, ls1_ref)
    checked_probabilities(0, *even, first=True)

    def pair(c, last):
        accumulate(c, *even)
        checked_probabilities(c + 1, *odd)
        accumulate(c + 1, *odd)
        if not last:
            checked_probabilities(c + 2, *even)

    def body(j, carry):
        pair(2 * j, last=False)
        return carry

    lax.fori_loop(0, n_chunks // 2 - 1, body, 0)
    pair(n_chunks - 2, last=True)
    denom = jnp.sum(l_ref[...], axis=0, keepdims=True)
    o_ref[...] = (acc_ref[...] / denom).T.astype(o_ref.dtype)


def _attention(qt, ka, vt, *, tq=2048, tk=2048):
    batch, heads, dqk, seq = qt.shape
    dv = vt.shape[2]
    assert seq % tq == 0 and seq % (2 * tk) == 0, "key chunks are consumed in pairs"
    return pl.pallas_call(
        functools.partial(_attn_kernel, tk),
        grid=(batch, heads, seq // tq),
        in_specs=[
            pl.BlockSpec((None, None, dqk, tq), lambda b, h, i: (b, h, 0, i)),
            pl.BlockSpec((None, None, seq, dqk), lambda b, h, i: (b, h, 0, 0)),
            pl.BlockSpec((None, None, dv, seq), lambda b, h, i: (b, h, 0, 0)),
        ],
        out_specs=pl.BlockSpec((None, tq, MLA_DV), lambda b, h, i: (b, i, h)),
        out_shape=jax.ShapeDtypeStruct((batch, seq, heads * MLA_DV), BF16),
        scratch_shapes=[pltpu.VMEM((dqk, tq), BF16), pltpu.VMEM((1, tq), F32),
                        pltpu.VMEM((dv, tq), F32), pltpu.VMEM((SUBLANES, tq), F32),
                        pltpu.VMEM((tk, tq), BF16), pltpu.VMEM((tk, tq), BF16),
                        pltpu.VMEM((SUBLANES, tq), F32), pltpu.VMEM((SUBLANES, tq), F32)],
        compiler_params=_cparams(("parallel", "parallel", "arbitrary")),
        name="mla_attn",
    )(qt, ka, vt)


def _outproj_kernel(h_ref, ys_ref, ym_ref, w_ref, o_ref):
    ws = ys_ref.shape[1]
    o_ref[...] = (h_ref[...]
                  + jnp.dot(ys_ref[...], w_ref[:ws, :], preferred_element_type=F32)
                  + jnp.dot(ym_ref[...], w_ref[ws:, :], preferred_element_type=F32))


def _outproj(h2d, ys, ym, w, layer, tm=512):
    m, d = h2d.shape
    return pl.pallas_call(
        _outproj_kernel,
        grid=(m // tm,),
        in_specs=[
            pl.BlockSpec((tm, d), lambda i: (i, 0)),
            pl.BlockSpec((tm, ys.shape[1]), lambda i: (i, 0)),
            pl.BlockSpec((tm, ym.shape[1]), lambda i: (i, 0)),
            pl.BlockSpec((None,) + w.shape[1:], lambda i: (layer, 0, 0)),
        ],
        out_specs=pl.BlockSpec((tm, d), lambda i: (i, 0)),
        out_shape=jax.ShapeDtypeStruct((m, d), F32),
        compiler_params=_cparams(("parallel",)),
        name="outproj",
    )(h2d, ys, ym, w)


def _ffn_kernel(final, h_ref, g_ref, gf_ref, w1_ref, w3_ref, w2_ref, o_ref, xn_ref, acc_ref):
    j = pl.program_id(1)

    @pl.when(j == 0)
    def _():
        xn_ref[...] = _rms(h_ref[...], g_ref[...]).astype(BF16)
        acc_ref[...] = jnp.zeros_like(acc_ref)

    xn = xn_ref[...]
    a = jnp.dot(xn, w1_ref[...], preferred_element_type=F32)
    b = jnp.dot(xn, w3_ref[...], preferred_element_type=F32)
    gated = (a * _sigmoid(a) * b).astype(BF16)
    acc_ref[...] += jnp.dot(gated, w2_ref[...], preferred_element_type=F32)

    @pl.when(j == pl.num_programs(1) - 1)
    def _():
        y = h_ref[...] + acc_ref[...]
        if final:
            y = _rms(y, gf_ref[...])
        o_ref[...] = y


def _ffn(h2d, gain, gain_final, w1, w3, w2, layer, *, final, tm=512, tf=512):
    m, d = h2d.shape
    f = w1.shape[2]
    return pl.pallas_call(
        functools.partial(_ffn_kernel, final),
        grid=(m // tm, f // tf),
        in_specs=[
            pl.BlockSpec((tm, d), lambda i, j: (i, 0)),
            pl.BlockSpec((1, d), lambda i, j: (0, 0)),
            pl.BlockSpec((1, d), lambda i, j: (0, 0)),
            pl.BlockSpec((None, d, tf), lambda i, j: (layer, 0, j)),
            pl.BlockSpec((None, d, tf), lambda i, j: (layer, 0, j)),
            pl.BlockSpec((None, tf, d), lambda i, j: (layer, j, 0)),
        ],
        out_specs=pl.BlockSpec((tm, d), lambda i, j: (i, 0)),
        out_shape=jax.ShapeDtypeStruct((m, d), F32),
        scratch_shapes=[pltpu.VMEM((tm, d), BF16), pltpu.VMEM((tm, d), F32)],
        compiler_params=_cparams(("parallel", "arbitrary")),
        name="ffn",
    )(h2d, gain, gain_final, w1, w3, w2)


def _cast_kernel(x_ref, o_ref):
    o_ref[...] = x_ref[...].astype(o_ref.dtype)


def _to_bf16(w, rows=512):
    w2d = w.reshape(-1, w.shape[-1])
    m, n = w2d.shape
    out = pl.pallas_call(
        _cast_kernel,
        grid=(m // rows,),
        in_specs=[pl.BlockSpec((rows, n), lambda i: (i, 0))],
        out_specs=pl.BlockSpec((rows, n), lambda i: (i, 0)),
        out_shape=jax.ShapeDtypeStruct((m, n), BF16),
        compiler_params=_cparams(("parallel",)),
        name="cast_bf16",
    )(w2d)
    return out.reshape(w.shape)


def _pad_heads(w, heads, dk):
    lead = w.shape[:-1]
    w = w.reshape(lead + (heads, dk))
    w = jnp.pad(w, [(0, 0)] * len(lead) + [(0, 0), (0, LANES - dk)])
    return w.reshape(lead + (heads * LANES,))


def _layout_w_in(w):
    w = w.astype(BF16)
    sizes = (256, 256, 512, 16, 16, 512, 512, 512, 512, 512, 512, 512, 512, 64)
    (gq, gk, gv, gaf, gab, gg, hq, hff, hfb, hi, hg, cq, ckv, kr) = jnp.split(
        w, np.cumsum(sizes)[:-1].tolist(), axis=1)
    half = MLA_ROPE // 2
    kr_swapped = jnp.concatenate([kr[:, half:], kr[:, :half]], axis=1)
    d = w.shape[0]
    cols = [_pad_heads(gq, GLA_HEADS, GLA_DK), _pad_heads(gk, GLA_HEADS, GLA_DK), gv, gg,
            hq, hff, hfb, hi, hg, cq, ckv, kr, kr_swapped,
            gaf, gab, jnp.zeros((d, LANES - 2 * GLA_GATE_RANK), w.dtype)]
    out = jnp.concatenate(cols, axis=1)
    return jnp.pad(out, [(0, 0), (0, U_WIDTH - out.shape[1])])


def _layout_w_qb(w):
    r = w.shape[0]
    w = w.astype(BF16).reshape(r, MLA_HEADS, MLA_DQK)
    rope = w[:, :, MLA_NOPE:]
    half = MLA_ROPE // 2
    swapped = jnp.concatenate([rope[:, :, half:], rope[:, :, :half]], axis=2)
    return jnp.concatenate([w, swapped], axis=2).reshape(r, MLA_HEADS * 256)


def _tri(tb, reverse):
    r = np.arange(tb)[:, None]
    c = np.arange(tb)[None, :]
    same = (r // SCAN_CHUNK) == (c // SCAN_CHUNK)
    keep = (c >= r) if reverse else (c <= r)
    return jnp.asarray((same & keep).astype(np.float32)).astype(BF16)


def kernel(x, positions, w_in, gla_gate_up, gla_gate_bias, gla_out_norm, hgrn_lb_logits,
           hgrn_out_norm, mla_qa_norm, mla_w_qb, mla_kva_norm, mla_w_kvb, w_out,
           norm_mix, norm_ffn, ffn_w1, ffn_w3, ffn_w2, norm_final):
    batch, seq, d = x.shape
    depth = w_in.shape[0]
    m = batch * seq

    inv_freq = 1.0 / (ROPE_THETA ** (jnp.arange(0, MLA_ROPE, 2, dtype=F32) / MLA_ROPE))
    ang = positions.astype(F32)[..., None] * inv_freq
    cos, sin = jnp.cos(ang), jnp.sin(ang)
    cs = jnp.concatenate([cos, cos, -sin, sin], axis=-1).reshape(m, LANES)

    tri_f, tri_b = _tri(SCAN_BLOCK, False), _tri(SCAN_BLOCK, True)
    w_out_b, w1_b, w3_b, w2_b = (_to_bf16(w) for w in (w_out, ffn_w1, ffn_w3, ffn_w2))
    h = x.reshape(m, d)
    for l in range(depth):
        u = _inproj(h, norm_mix[l].reshape(1, d), _layout_w_in(w_in[l]))

        def gate_w(direction):
            wgd = _pad_heads(gla_gate_up[l, direction], GLA_HEADS, GLA_DK)
            lo = direction * GLA_GATE_RANK
            return jnp.pad(wgd, [(lo, LANES - lo - GLA_GATE_RANK), (0, 0)])

        gb = _pad_heads(gla_gate_bias[l], GLA_HEADS, GLA_DK)
        onorm = jnp.concatenate([jnp.tile(gla_out_norm[l], GLA_HEADS),
                                 jnp.tile(hgrn_out_norm[l], HGRN_HEADS)]).reshape(1, SCAN_WIDTH)
        o_b = _scan(u, None, gate_w(1), gb[1:2], hgrn_lb_logits[1], tri_b, None,
                    batch=batch, seq=seq, layer=l, reverse=True)
        y_scan = _scan(u, o_b, gate_w(0), gb[0:1], hgrn_lb_logits[0], tri_f, onorm,
                       batch=batch, seq=seq, layer=l, reverse=False)

        q, k, vt = _mla_prep(u, cs, mla_qa_norm[l].reshape(1, -1), mla_kva_norm[l].reshape(1, -1),
                             _layout_w_qb(mla_w_qb[l]), mla_w_kvb[l].astype(BF16),
                             batch=batch, seq=seq)
        y_mla = _attention(q, k, vt).reshape(m, MLA_HEADS * MLA_DV)

        h = _outproj(h, y_scan, y_mla, w_out_b, l)
        h = _ffn(h, norm_ffn[l].reshape(1, d), norm_final.reshape(1, d), w1_b, w3_b, w2_b, l,
                 final=(l == depth - 1))
    return h.reshape(batch, seq, d)
```
